```python
import math
import jax, jax.numpy as jnp
from jax import lax
import numpy as np

D_MODEL = 1024
BATCH = 4
SEQ = 8192
DEPTH = 4
DEC_BATCH = 2
DEC_SEQ = 8192
PAST_LEN = 128

MLA_HEADS = 8
MLA_Q_LORA = 256
MLA_KV_LORA = 128
MLA_NOPE = 64
MLA_ROPE = 32
MLA_V = 64
MLA_QK = MLA_NOPE + MLA_ROPE
MLA_THETA = 10000.0
DIFF_HEADS = 4
DIFF_DK = 64
DIFF_DV = 2 * DIFF_DK
DIFF_ROPE = DIFF_DK // 4
ROPE_THETA = 500000.0
MIX_WIDTH = MLA_HEADS * MLA_V + DIFF_HEADS * DIFF_DV
IN_SIZES = (MLA_Q_LORA, MLA_KV_LORA, MLA_ROPE,
            DIFF_HEADS * 2 * DIFF_DK, DIFF_HEADS * 2 * DIFF_DK, DIFF_HEADS * DIFF_DV)
IN_DIM = sum(IN_SIZES)
IN_SPLITS = tuple(int(v) for v in np.cumsum(IN_SIZES)[:-1])
D_FF = 2816
CONV_W = 3
Q_BLOCK = 128
EPS = 1e-6

kernel_name = "hybrid_mla_diffattn_convffn_encoder"


def _rms_norm(x, g):
    xf = x.astype(jnp.float32)
    y = xf * lax.rsqrt(jnp.mean(xf * xf, axis=-1, keepdims=True) + EPS)
    return (y * g.astype(jnp.float32)).astype(x.dtype)


def _rope(x, theta):
    s, d = x.shape[1], x.shape[-1]
    half = d // 2
    freqs = theta ** (-jnp.arange(half, dtype=jnp.float32) * 2.0 / d)
    ang = jnp.arange(s, dtype=jnp.float32)[:, None] * freqs[None, :]
    shape = (1, s) + (1,) * (x.ndim - 3) + (half,)
    cos = jnp.cos(ang).reshape(shape)
    sin = jnp.sin(ang).reshape(shape)
    xf = x.astype(jnp.float32)
    x1, x2 = xf[..., :half], xf[..., half:]
    return jnp.concatenate([x1 * cos - x2 * sin, x2 * cos + x1 * sin], axis=-1).astype(x.dtype)


def _over_query_blocks(fn, q):
    b, s = q.shape[:2]
    nb = s // Q_BLOCK
    qb = jnp.moveaxis(q.reshape((b, nb, Q_BLOCK) + q.shape[2:]), 1, 0)
    ob = jnp.moveaxis(lax.map(fn, qb), 0, 1)
    return ob.reshape((b, s) + ob.shape[3:])


def _mla_attention(q, k, v):
    scale = MLA_QK ** -0.5

    def blk(qi):
        sc = jnp.einsum('bqhd,bkhd->bhqk', qi, k).astype(jnp.float32) * scale
        p = jax.nn.softmax(sc, axis=-1).astype(v.dtype)
        return jnp.einsum('bhqk,bkhd->bqhd', p, v)

    return _over_query_blocks(blk, q)


def _diff_attention(q, k, v, lam):
    scale = DIFF_DK ** -0.5

    def blk(qi):
        sc = jnp.einsum('bqhmd,bkhmd->bhmqk', qi, k).astype(jnp.float32) * scale
        p = jax.nn.softmax(sc, axis=-1)
        a = (p[:, :, 0] - lam * p[:, :, 1]).astype(v.dtype)
        return jnp.einsum('bhqk,bkhd->bqhd', a, v)

    return _over_query_blocks(blk, q)


def _dwconv3(h, w, b):
    hp = jnp.pad(h, ((0, 0), (1, 1), (0, 0)))
    return hp[:, :-2] * w[0] + hp[:, 1:-1] * w[1] + hp[:, 2:] * w[2] + b


def _layer(x, l, p):
    b, s, _ = x.shape
    h = _rms_norm(x, p['ln1_g'])
    proj = h @ p['w_in']
    c_q, c_kv, k_pe, dq, dk, dv = jnp.split(proj, IN_SPLITS, axis=-1)

    q = (_rms_norm(c_q, p['mla_q_norm_g']) @ p['w_q_up']).reshape(b, s, MLA_HEADS, MLA_QK)
    kv = (_rms_norm(c_kv, p['mla_kv_norm_g']) @ p['w_kv_up']).reshape(b, s, MLA_HEADS, MLA_NOPE + MLA_V)
    k_nope, v_m = kv[..., :MLA_NOPE], kv[..., MLA_NOPE:]
    k_m = jnp.concatenate(
        [k_nope, jnp.broadcast_to(k_pe[:, :, None, :], (b, s, MLA_HEADS, MLA_ROPE))], axis=-1)
    q = _rms_norm(q, p['mla_qn_g'])
    k_m = _rms_norm(k_m, p['mla_kn_g'])
    q = jnp.concatenate([q[..., :MLA_NOPE], _rope(q[..., MLA_NOPE:], MLA_THETA)], axis=-1)
    k_m = jnp.concatenate([k_m[..., :MLA_NOPE], _rope(k_m[..., MLA_NOPE:], MLA_THETA)], axis=-1)
    o_mla = _mla_attention(q, k_m, v_m).reshape(b, s, MLA_HEADS * MLA_V)

    qd = _rms_norm(dq.reshape(b, s, DIFF_HEADS, 2, DIFF_DK), p['diff_qn_g'])
    kd = _rms_norm(dk.reshape(b, s, DIFF_HEADS, 2, DIFF_DK), p['diff_kn_g'])
    vd = dv.reshape(b, s, DIFF_HEADS, DIFF_DV)
    qd = jnp.concatenate([_rope(qd[..., :DIFF_ROPE], ROPE_THETA), qd[..., DIFF_ROPE:]], axis=-1)
    kd = jnp.concatenate([_rope(kd[..., :DIFF_ROPE], ROPE_THETA), kd[..., DIFF_ROPE:]], axis=-1)
    lam_init = 0.8 - 0.6 * math.exp(-0.3 * l)
    f32 = jnp.float32
    lam = (jnp.exp(jnp.sum(p['lambda_q1'].astype(f32) * p['lambda_k1'].astype(f32)))
           - jnp.exp(jnp.sum(p['lambda_q2'].astype(f32) * p['lambda_k2'].astype(f32)))
           + lam_init)
    o_d = _diff_attention(qd, kd, vd, lam)
    o_d = (_rms_norm(o_d, p['diff_subln_g']) * (1.0 - lam_init)).reshape(b, s, DIFF_HEADS * DIFF_DV)

    x = x + jnp.concatenate([o_mla, o_d], axis=-1) @ p['w_out']

    h = _rms_norm(x, p['ln2_g'])
    g = _dwconv3(h @ p['w_gate'], p['conv_w'], p['conv_b'])
    x = x + (jax.nn.silu(g) * (h @ p['w_up'])) @ p['w_down']
    return x


def _trunk(x, params):
    for l in range(DEPTH):
        x = _layer(x, l, {name: arr[l] for name, arr in params.items()})
    return x


def setup_inputs(seed: int = 0) -> dict:
    key = jax.random.key(seed)
    ks = jax.random.split(key, 28)

    def nrm(k, shape, scale):
        return jax.random.normal(k, shape, jnp.float32) * scale

    def gain(k, n):
        return 1.0 + nrm(k, (DEPTH, n), 0.02)

    return {
        'x_prompt': nrm(ks[0], (BATCH, SEQ, D_MODEL), 1.0),
        'x_sample': nrm(ks[1], (DEC_BATCH, DEC_SEQ, D_MODEL), 1.0),
        'ln1_g': gain(ks[2], D_MODEL),
        'w_in': nrm(ks[3], (DEPTH, D_MODEL, IN_DIM), D_MODEL ** -0.5),
        'mla_q_norm_g': gain(ks[4], MLA_Q_LORA),
        'w_q_up': nrm(ks[5], (DEPTH, MLA_Q_LORA, MLA_HEADS * MLA_QK), MLA_Q_LORA ** -0.5),
        'mla_kv_norm_g': gain(ks[6], MLA_KV_LORA),
        'w_kv_up': nrm(ks[7], (DEPTH, MLA_KV_LORA, MLA_HEADS * (MLA_NOPE + MLA_V)), MLA_KV_LORA ** -0.5),
        'mla_qn_g': gain(ks[8], MLA_QK),
        'mla_kn_g': gain(ks[9], MLA_QK),
        'diff_qn_g': gain(ks[10], DIFF_DK),
        'diff_kn_g': gain(ks[11], DIFF_DK),
        'lambda_q1': nrm(ks[12], (DEPTH, DIFF_DK), 0.1),
        'lambda_k1': nrm(ks[13], (DEPTH, DIFF_DK), 0.1),
        'lambda_q2': nrm(ks[14], (DEPTH, DIFF_DK), 0.1),
        'lambda_k2': nrm(ks[15], (DEPTH, DIFF_DK), 0.1),
        'diff_subln_g': gain(ks[16], DIFF_DV),
        'w_out': nrm(ks[17], (DEPTH, MIX_WIDTH, D_MODEL), 0.5 * MIX_WIDTH ** -0.5),
        'ln2_g': gain(ks[18], D_MODEL),
        'w_gate': nrm(ks[19], (DEPTH, D_MODEL, D_FF), D_MODEL ** -0.5),
        'conv_w': nrm(ks[20], (DEPTH, CONV_W, D_FF), CONV_W ** -0.5),
        'conv_b': nrm(ks[21], (DEPTH, D_FF), 0.01),
        'w_up': nrm(ks[22], (DEPTH, D_MODEL, D_FF), D_MODEL ** -0.5),
        'w_down': nrm(ks[23], (DEPTH, D_FF, D_MODEL), 0.5 * D_FF ** -0.5),
    }


def reference(x_prompt, x_sample, ln1_g, w_in, mla_q_norm_g, w_q_up, mla_kv_norm_g, w_kv_up,
              mla_qn_g, mla_kn_g, diff_qn_g, diff_kn_g, lambda_q1, lambda_k1, lambda_q2,
              lambda_k2, diff_subln_g, w_out, ln2_g, w_gate, conv_w, conv_b, w_up, w_down):
    params = dict(ln1_g=ln1_g, w_in=w_in, mla_q_norm_g=mla_q_norm_g, w_q_up=w_q_up,
                  mla_kv_norm_g=mla_kv_norm_g, w_kv_up=w_kv_up, mla_qn_g=mla_qn_g,
                  mla_kn_g=mla_kn_g, diff_qn_g=diff_qn_g, diff_kn_g=diff_kn_g,
                  lambda_q1=lambda_q1, lambda_k1=lambda_k1, lambda_q2=lambda_q2,
                  lambda_k2=lambda_k2, diff_subln_g=diff_subln_g, w_out=w_out, ln2_g=ln2_g,
                  w_gate=w_gate, conv_w=conv_w, conv_b=conv_b, w_up=w_up, w_down=w_down)
    y_prompt = _trunk(x_prompt, params)
    y_sample = _trunk(x_sample, params)
    return (y_prompt, y_sample)
```

```python
import functools
import math

import jax
import jax.numpy as jnp
import numpy as np
from jax import lax
from jax.experimental import pallas as pl
from jax.experimental.pallas import tpu as pltpu

D_MODEL = 1024
DEPTH = 4
MLA_HEADS = 8
MLA_Q_LORA = 256
MLA_KV_LORA = 128
MLA_NOPE = 64
MLA_ROPE = 32
MLA_V = 64
MLA_QK = MLA_NOPE + MLA_ROPE
MLA_THETA = 10000.0
DIFF_HEADS = 4
DIFF_DK = 64
DIFF_DV = 2 * DIFF_DK
DIFF_ROPE = DIFF_DK // 4
ROPE_THETA = 500000.0
D_FF = 2816
EPS = 1e-6

LANES = 128
MXU_DTYPE = jnp.bfloat16
VMEM_LIMIT = 56 * 1024 * 1024

PRE_TM = 512
ATT_TQ = 1024
ATT_TK = 256
FFN_TM = 512
FFN_HALO = 16
FFN_CHUNK = 256
NEG_BIG = -1e30

_NT = (((1,), (1,)), ((), ()))


def _dot(a, b):
    return jnp.dot(a, b, preferred_element_type=jnp.float32)


def _dot_nt(a, b):
    return lax.dot_general(a, b, _NT, preferred_element_type=jnp.float32)


def _rms(x, g):
    ms = jnp.mean(x * x, axis=-1, keepdims=True)
    return x * lax.rsqrt(ms + EPS) * g


def _group_rsqrt(xc, block_ones, width):
    ssq = _dot((xc * xc).astype(MXU_DTYPE), block_ones)
    return lax.rsqrt(ssq * (1.0 / width) + EPS)


def _rope_group(xg, cos, sin_a, sin_b, shift):
    return (xg * cos + pltpu.roll(xg, LANES - shift, 1) * sin_a
            + pltpu.roll(xg, shift, 1) * sin_b)


def _pre_kernel(x_ref, ln1_ref, win_ref, wdvt_ref, gcq_ref, wq_ref, gckv_ref, wk_ref, wvt_ref,
                gqn_ref, gkn_ref, gdq_ref, gdk_ref,
                cosm_ref, sma_ref, smb_ref, cosd_ref, sda_ref, sdb_ref,
                bo128_ref, bo64_ref,
                qm_ref, km_ref, vmt_ref, qd_ref, kd_ref, vdt_ref):
    tm = x_ref.shape[1]
    tk = vmt_ref.shape[3]
    x = x_ref[0]
    hb = _rms(x, ln1_ref[...]).astype(MXU_DTYPE)
    proj = _dot(hb, win_ref[...])
    c_q = proj[:, 0:256]
    c_kv = proj[:, 256:384]
    kpe = proj[:, 384:512]
    dq = proj[:, 512:1024]
    dk = proj[:, 1024:1536]

    dvt = _dot_nt(wdvt_ref[...], hb).astype(MXU_DTYPE)
    cqn = _rms(c_q, gcq_ref[...]).astype(MXU_DTYPE)
    ckvn = _rms(c_kv, gckv_ref[...]).astype(MXU_DTYPE)
    q = _dot(cqn, wq_ref[...])
    kn = _dot(ckvn, wk_ref[...])
    vt = _dot_nt(wvt_ref[...], ckvn).astype(MXU_DTYPE)
    for t in range(tm // tk):
        vmt_ref[0, t] = vt[:, t * tk:(t + 1) * tk]
        vdt_ref[0, t] = dvt[:, t * tk:(t + 1) * tk]

    bo128 = bo128_ref[...]
    bo64 = bo64_ref[...]
    cosm, sma, smb = cosm_ref[...], sma_ref[...], smb_ref[...]
    cosd, sda, sdb = cosd_ref[...], sda_ref[...], sdb_ref[...]
    gqn, gkn, gdq, gdk = gqn_ref[...], gkn_ref[...], gdq_ref[...], gdk_ref[...]
    kpe2 = jnp.concatenate([kpe, kpe], axis=1)
    q_scale = MLA_QK ** -0.5
    d_scale = DIFF_DK ** -0.5
    half_m = MLA_ROPE // 2
    half_d = DIFF_ROPE // 2
    lane = lax.broadcasted_iota(jnp.int32, (tm, LANES), 1)

    for c in range(MLA_HEADS // 2):
        sl = slice(c * 256, (c + 1) * 256)
        qc = q[:, sl]
        qc = qc * _group_rsqrt(qc, bo128, MLA_QK)
        kc = kn[:, sl] + kpe2
        kc = kc * _group_rsqrt(kc, bo128, MLA_QK)
        for g in range(2):
            gs = slice(g * LANES, (g + 1) * LANES)
            out = slice((2 * c + g) * LANES, (2 * c + g + 1) * LANES)
            qg = _rope_group(qc[:, gs] * gqn, cosm, sma, smb, half_m)
            qm_ref[0, :, out] = (qg * q_scale).astype(MXU_DTYPE)
            kg = _rope_group(kc[:, gs] * gkn, cosm, sma, smb, half_m)
            km_ref[0, :, out] = kg.astype(MXU_DTYPE)

    for c in range(DIFF_HEADS // 2):
        sl = slice(c * 256, (c + 1) * 256)
        qc = dq[:, sl]
        qc = qc * _group_rsqrt(qc, bo64, DIFF_DK)
        kc = dk[:, sl]
        kc = kc * _group_rsqrt(kc, bo64, DIFF_DK)
        for g in range(2):
            h = 2 * c + g
            gs = slice(g * LANES, (g + 1) * LANES)
            qg = _rope_group(qc[:, gs] * gdq, cosd, sda, sdb, half_d) * d_scale
            zero = jnp.zeros_like(qg)
            qd_ref[0, :, (2 * h) * LANES:(2 * h + 1) * LANES] = (
                jnp.where(lane < DIFF_DK, qg, zero).astype(MXU_DTYPE))
            qd_ref[0, :, (2 * h + 1) * LANES:(2 * h + 2) * LANES] = (
                jnp.where(lane >= DIFF_DK, qg, zero).astype(MXU_DTYPE))
            kg = _rope_group(kc[:, gs] * gdk, cosd, sda, sdb, half_d)
            kd_ref[0, :, h * LANES:(h + 1) * LANES] = kg.astype(MXU_DTYPE)


def _pre_call(x, lp, tabs, tm, tk):
    b, s, d = x.shape
    nt = s // tm
    tpb = tm // tk

    def full(a):
        return pl.BlockSpec(a.shape, lambda bi, ti: (0,) * a.ndim)

    def tab_spec():
        return pl.BlockSpec((tm, LANES), lambda bi, ti: (ti, 0))

    weights = [lp['ln1'], lp['w_in'], lp['w_dvt'], lp['g_cq'], lp['w_q'], lp['g_ckv'], lp['w_k'],
               lp['w_vt'], lp['g_qn'], lp['g_kn'], lp['g_dq'], lp['g_dk']]
    consts = [tabs['bo128'], tabs['bo64']]
    rope = [tabs['cosm'], tabs['sma'], tabs['smb'], tabs['cosd'], tabs['sda'], tabs['sdb']]
    in_specs = ([pl.BlockSpec((1, tm, d), lambda bi, ti: (bi, ti, 0))]
                + [full(a) for a in weights] + [tab_spec() for _ in rope] + [full(a) for a in consts])
    tok = lambda w: pl.BlockSpec((1, tm, w), lambda bi, ti: (bi, ti, 0))
    vts = pl.BlockSpec((1, tpb, 512, tk), lambda bi, ti: (bi, ti, 0, 0))
    out_shape = [
        jax.ShapeDtypeStruct((b, s, 1024), MXU_DTYPE),
        jax.ShapeDtypeStruct((b, s, 1024), MXU_DTYPE),
        jax.ShapeDtypeStruct((b, s // tk, 512, tk), MXU_DTYPE),
        jax.ShapeDtypeStruct((b, s, 1024), MXU_DTYPE),
        jax.ShapeDtypeStruct((b, s, 512), MXU_DTYPE),
        jax.ShapeDtypeStruct((b, s // tk, 512, tk), MXU_DTYPE),
    ]
    out_specs = [tok(1024), tok(1024), vts, tok(1024), tok(512), vts]
    return pl.pallas_call(
        _pre_kernel,
        grid=(b, nt),
        in_specs=in_specs,
        out_specs=out_specs,
        out_shape=out_shape,
        compiler_params=pltpu.CompilerParams(
            dimension_semantics=("arbitrary", "arbitrary"), vmem_limit_bytes=VMEM_LIMIT),
        name="proj",
    )(x, *weights, *rope, *consts)


def _flash_step(k, q, vt_ext, m, acc):
    s = _dot_nt(k, q)
    m_new = jnp.maximum(m, jnp.max(s, axis=0, keepdims=True))
    p = jnp.exp(s - m_new).astype(MXU_DTYPE)
    alpha = jnp.exp(m - m_new)
    acc = acc * alpha + _dot(vt_ext, p)
    return m_new, acc


def _mla_attn_kernel(q_ref, k_ref, vt_ref, o_ref, *, tq, tk):
    s_len = q_ref.shape[1]
    nq, nk = s_len // tq, s_len // tk
    ones = jnp.ones((16, tk), MXU_DTYPE)
    dv = MLA_V

    def qbody(qi, c):
        qoff = pl.multiple_of(qi * tq, tq)
        qs = [q_ref[0, pl.ds(qoff, tq), h * LANES:(h + 1) * LANES] for h in range(2)]

        def kbody(j, carry):
            koff = pl.multiple_of(j * tk, tk)
            vt_all = vt_ref[0, j]
            out = []
            for h in range(2):
                k = k_ref[0, pl.ds(koff, tk), h * LANES:(h + 1) * LANES]
                vt = jnp.concatenate([vt_all[h * dv:(h + 1) * dv], ones], axis=0)
                out.append(_flash_step(k, qs[h], vt, *carry[h]))
            return tuple(out)

        init = tuple((jnp.full((1, tq), NEG_BIG, jnp.float32),
                      jnp.zeros((dv + 16, tq), jnp.float32)) for _ in range(2))
        res = lax.fori_loop(0, nk, kbody, init)
        o = jnp.concatenate([res[h][1][:dv] / res[h][1][dv:dv + 1] for h in range(2)], axis=0)
        o_ref[0, pl.ds(qoff, tq), :] = o.T.astype(o_ref.dtype)
        return c

    lax.fori_loop(0, nq, qbody, 0)


def _mla_attn_call(qm, km, vmt, tq, tk):
    b, s, _ = qm.shape
    kern = functools.partial(_mla_attn_kernel, tq=tq, tk=tk)
    return pl.pallas_call(
        kern,
        grid=(b, MLA_HEADS // 2),
        in_specs=[
            pl.BlockSpec((1, s, 2 * LANES), lambda bi, hp: (bi, 0, hp)),
            pl.BlockSpec((1, s, 2 * LANES), lambda bi, hp: (bi, 0, hp)),
            pl.BlockSpec((1, s // tk, 2 * MLA_V, tk), lambda bi, hp: (bi, 0, hp, 0)),
        ],
        out_specs=pl.BlockSpec((1, s, 2 * MLA_V), lambda bi, hp: (bi, 0, hp)),
        out_shape=jax.ShapeDtypeStruct((b, s, MLA_HEADS * MLA_V), MXU_DTYPE),
        compiler_params=pltpu.CompilerParams(
            dimension_semantics=("arbitrary", "arbitrary"), vmem_limit_bytes=VMEM_LIMIT),
        name="mla_attn",
    )(qm, km, vmt)


def _diff_attn_kernel(q_ref, k_ref, vt_ref, lq1_ref, lk1_ref, lq2_ref, lk2_ref, lam0_ref, g_ref,
                      o_ref, *, tq, tk):
    s_len = q_ref.shape[1]
    nq, nk = s_len // tq, s_len // tk
    ones = jnp.ones((16, tk), MXU_DTYPE)
    dv = DIFF_DV
    lam0 = lam0_ref[...]
    lam = (jnp.exp(jnp.sum(lq1_ref[...] * lk1_ref[...], axis=-1, keepdims=True))
           - jnp.exp(jnp.sum(lq2_ref[...] * lk2_ref[...], axis=-1, keepdims=True)) + lam0)
    gain = g_ref[...] * (1.0 - lam0)

    def qbody(qi, c):
        qoff = pl.multiple_of(qi * tq, tq)
        qs = [q_ref[0, pl.ds(qoff, tq), mi * LANES:(mi + 1) * LANES] for mi in range(2)]

        def kbody(j, carry):
            koff = pl.multiple_of(j * tk, tk)
            k = k_ref[0, pl.ds(koff, tk), :]
            vt = jnp.concatenate([vt_ref[0, j], ones], axis=0)
            return tuple(_flash_step(k, qs[mi], vt, *carry[mi]) for mi in range(2))

        init = tuple((jnp.full((1, tq), NEG_BIG, jnp.float32),
                      jnp.zeros((dv + 16, tq), jnp.float32)) for _ in range(2))
        res = lax.fori_loop(0, nk, kbody, init)
        o1 = res[0][1][:dv] / res[0][1][dv:dv + 1]
        o2 = res[1][1][:dv] / res[1][1][dv:dv + 1]
        o = o1 - lam * o2
        ms = jnp.mean(o * o, axis=0, keepdims=True)
        y = o * lax.rsqrt(ms + EPS) * gain
        o_ref[0, pl.ds(qoff, tq), :] = y.T.astype(o_ref.dtype)
        return c

    lax.fori_loop(0, nq, qbody, 0)


def _diff_attn_call(qd, kd, vdt, lp, tq, tk):
    b, s, _ = qd.shape
    kern = functools.partial(_diff_attn_kernel, tq=tq, tk=tk)
    small = [lp['lq1'], lp['lk1'], lp['lq2'], lp['lk2'], lp['lam0'], lp['g_sub']]
    return pl.pallas_call(
        kern,
        grid=(b, DIFF_HEADS),
        in_specs=[
            pl.BlockSpec((1, s, 2 * LANES), lambda bi, h: (bi, 0, h)),
            pl.BlockSpec((1, s, LANES), lambda bi, h: (bi, 0, h)),
            pl.BlockSpec((1, s // tk, DIFF_DV, tk), lambda bi, h: (bi, 0, h, 0)),
        ] + [pl.BlockSpec(a.shape, lambda bi, h: (0, 0)) for a in small],
        out_specs=pl.BlockSpec((1, s, DIFF_DV), lambda bi, h: (bi, 0, h)),
        out_shape=jax.ShapeDtypeStruct((b, s, DIFF_HEADS * DIFF_DV), MXU_DTYPE),
        compiler_params=pltpu.CompilerParams(
            dimension_semantics=("arbitrary", "arbitrary"), vmem_limit_bytes=VMEM_LIMIT),
        name="diff_attn",
    )(qd, kd, vdt, *small)


def _ffn_kernel(x_ref, xp_ref, xn_ref, am_ref, amp_ref, amn_ref, ad_ref, adp_ref, adn_ref,
                wom_ref, wod_ref, ln2_ref, wg_ref, wu_ref, cw_ref, wd_ref,
                o_ref, h_scr, g_scr):
    tm = x_ref.shape[1]
    halo = xp_ref.shape[1]
    ti = pl.program_id(1)
    nt = pl.num_programs(1)
    nchunk = wg_ref.shape[0]

    x_ext = jnp.concatenate([xp_ref[0], x_ref[0], xn_ref[0]], axis=0)
    am_ext = jnp.concatenate([amp_ref[0], am_ref[0], amn_ref[0]], axis=0)
    ad_ext = jnp.concatenate([adp_ref[0], ad_ref[0], adn_ref[0]], axis=0)
    xmid = x_ext + _dot(am_ext, wom_ref[...]) + _dot(ad_ext, wod_ref[...])
    h_scr[...] = _rms(xmid, ln2_ref[...]).astype(h_scr.dtype)
    o_ref[0] = xmid[halo:halo + tm]

    row = lax.broadcasted_iota(jnp.int32, (tm + 2 * halo, 1), 0)
    valid = jnp.logical_and(jnp.logical_or(row >= halo, ti > 0),
                            jnp.logical_or(row < halo + tm, ti < nt - 1))

    def chunk(c, carry):
        h_ext = h_scr[...]
        g = _dot(h_ext, wg_ref[c])
        g_scr[...] = jnp.where(valid, g, 0.0)
        u = _dot(h_ext[halo:halo + tm], wu_ref[c])
        cw = cw_ref[c]
        conv = (g_scr[pl.ds(halo - 1, tm), :] * cw[0:1] + g_scr[pl.ds(halo, tm), :] * cw[1:2]
                + g_scr[pl.ds(halo + 1, tm), :] * cw[2:3] + cw[3:4])
        a = (conv * jax.nn.sigmoid(conv) * u).astype(h_scr.dtype)
        o_ref[0] += _dot(a, wd_ref[c])
        return carry

    lax.fori_loop(0, nchunk, chunk, 0)


def _ffn_call(x, am, ad, lp, tm):
    b, s, d = x.shape
    nt = s // tm
    halo = FFN_HALO
    r = tm // halo
    nhb = s // halo

    def main(w):
        return pl.BlockSpec((1, tm, w), lambda bi, ti: (bi, ti, 0))

    def prev(w):
        return pl.BlockSpec((1, halo, w), lambda bi, ti: (bi, jnp.maximum(ti * r - 1, 0), 0))

    def nxt(w):
        return pl.BlockSpec((1, halo, w), lambda bi, ti: (bi, jnp.minimum((ti + 1) * r, nhb - 1), 0))

    def resident(a):
        return pl.BlockSpec(a.shape, lambda bi, ti: (0,) * a.ndim, pipeline_mode=pl.Buffered(1))

    weights = [lp['wo_m'], lp['wo_d'], lp['ln2'], lp['w_gate'], lp['w_up'], lp['conv'], lp['w_down']]
    wa = am.shape[-1]
    in_specs = ([main(d), prev(d), nxt(d), main(wa), prev(wa), nxt(wa), main(wa), prev(wa), nxt(wa)]
                + [resident(a) for a in weights])
    return pl.pallas_call(
        _ffn_kernel,
        grid=(b, nt),
        in_specs=in_specs,
        out_specs=main(d),
        out_shape=jax.ShapeDtypeStruct((b, s, d), jnp.float32),
        scratch_shapes=[pltpu.VMEM((tm + 2 * halo, d), MXU_DTYPE),
                        pltpu.VMEM((tm + 2 * halo, FFN_CHUNK), jnp.float32)],
        compiler_params=pltpu.CompilerParams(
            dimension_semantics=("arbitrary", "arbitrary"), vmem_limit_bytes=VMEM_LIMIT),
        name="ffn",
    )(x, x, x, am, am, am, ad, ad, ad, *weights)


def _rope_tables(s, half, theta, lane_starts):
    d = 2 * half
    freqs = theta ** (-jnp.arange(half, dtype=jnp.float32) * 2.0 / d)
    ang = jnp.arange(s, dtype=jnp.float32)[:, None] * freqs[None, :]
    cos, sin = jnp.cos(ang), jnp.sin(ang)
    cos_t = jnp.ones((s, LANES), jnp.float32)
    sa = jnp.zeros((s, LANES), jnp.float32)
    sb = jnp.zeros((s, LANES), jnp.float32)
    for st in lane_starts:
        cos_t = cos_t.at[:, st:st + half].set(cos).at[:, st + half:st + d].set(cos)
        sa = sa.at[:, st:st + half].set(-sin)
        sb = sb.at[:, st + half:st + d].set(sin)
    return cos_t, sa, sb


def _block_ones(width):
    idx = np.arange(256) // width
    return jnp.asarray(idx[:, None] == idx[None, :], MXU_DTYPE)


def _tables(s):
    cosm, sma, smb = _rope_tables(s, MLA_ROPE // 2, MLA_THETA, (MLA_NOPE,))
    cosd, sda, sdb = _rope_tables(s, DIFF_ROPE // 2, ROPE_THETA, (0, DIFF_DK))
    return dict(cosm=cosm, sma=sma, smb=smb, cosd=cosd, sda=sda, sdb=sdb,
                bo128=_block_ones(LANES), bo64=_block_ones(DIFF_DK))


def _layer_params(l, p):
    f32 = jnp.float32
    cd = MXU_DTYPE
    w_in = p['w_in'][l]
    o_cq, o_ckv, o_kpe = 0, MLA_Q_LORA, MLA_Q_LORA + MLA_KV_LORA
    o_dq = o_kpe + MLA_ROPE
    o_dk = o_dq + DIFF_HEADS * 2 * DIFF_DK
    o_dv = o_dk + DIFF_HEADS * 2 * DIFF_DK
    kpe_pad = jnp.zeros((D_MODEL, LANES), f32).at[:, MLA_NOPE:MLA_QK].set(w_in[:, o_kpe:o_dq])
    w_in_main = jnp.concatenate([w_in[:, o_cq:o_kpe], kpe_pad, w_in[:, o_dq:o_dv]], axis=1)
    w_q = p['w_q_up'][l].reshape(MLA_Q_LORA, MLA_HEADS, MLA_QK)
    w_q = jnp.pad(w_q, ((0, 0), (0, 0), (0, LANES - MLA_QK))).reshape(MLA_Q_LORA, MLA_HEADS * LANES)
    w_kv = p['w_kv_up'][l].reshape(MLA_KV_LORA, MLA_HEADS, MLA_NOPE + MLA_V)
    w_k = jnp.pad(w_kv[:, :, :MLA_NOPE], ((0, 0), (0, 0), (0, LANES - MLA_NOPE)))
    w_k = w_k.reshape(MLA_KV_LORA, MLA_HEADS * LANES)
    w_vt = w_kv[:, :, MLA_NOPE:].reshape(MLA_KV_LORA, MLA_HEADS * MLA_V).T
    nchunk = D_FF // FFN_CHUNK
    conv = jnp.concatenate([p['conv_w'][l], p['conv_b'][l][None, :],
                            jnp.zeros((4, D_FF), f32)], axis=0)
    lam0 = 0.8 - 0.6 * math.exp(-0.3 * l)
    row = lambda v: v.astype(f32)[None, :]
    w_out = p['w_out'][l]
    return dict(
        ln1=row(p['ln1_g'][l]),
        w_in=w_in_main.astype(cd),
        w_dvt=w_in[:, o_dv:].T.astype(cd),
        g_cq=row(p['mla_q_norm_g'][l]),
        w_q=w_q.astype(cd),
        g_ckv=row(p['mla_kv_norm_g'][l]),
        w_k=w_k.astype(cd),
        w_vt=w_vt.astype(cd),
        g_qn=row(jnp.pad(p['mla_qn_g'][l], (0, LANES - MLA_QK))),
        g_kn=row(jnp.pad(p['mla_kn_g'][l], (0, LANES - MLA_QK))),
        g_dq=row(jnp.tile(p['diff_qn_g'][l], 2)),
        g_dk=row(jnp.tile(p['diff_kn_g'][l], 2)),
        lq1=row(p['lambda_q1'][l]), lk1=row(p['lambda_k1'][l]),
        lq2=row(p['lambda_q2'][l]), lk2=row(p['lambda_k2'][l]),
        lam0=jnp.full((1, 1), lam0, f32),
        g_sub=p['diff_subln_g'][l].astype(f32)[:, None],
        wo_m=w_out[:MLA_HEADS * MLA_V].astype(cd),
        wo_d=w_out[MLA_HEADS * MLA_V:].astype(cd),
        ln2=row(p['ln2_g'][l]),
        w_gate=p['w_gate'][l].reshape(D_MODEL, nchunk, FFN_CHUNK).transpose(1, 0, 2).astype(cd),
        w_up=p['w_up'][l].reshape(D_MODEL, nchunk, FFN_CHUNK).transpose(1, 0, 2).astype(cd),
        conv=conv.reshape(8, nchunk, FFN_CHUNK).transpose(1, 0, 2),
        w_down=p['w_down'][l].reshape(nchunk, FFN_CHUNK, D_MODEL).astype(cd),
    )


def _trunk(x, params):
    b, s, _ = x.shape
    tq = min(ATT_TQ, s)
    tk = min(ATT_TK, s)
    tm_pre = min(PRE_TM, s)
    tm_ffn = min(FFN_TM, s)
    tabs = _tables(s)
    for l in range(DEPTH):
        lp = _layer_params(l, params)
        qm, km, vmt, qd, kd, vdt = _pre_call(x, lp, tabs, tm_pre, tk)
        am = _mla_attn_call(qm, km, vmt, tq, tk)
        ad = _diff_attn_call(qd, kd, vdt, lp, tq, tk)
        x = _ffn_call(x, am, ad, lp, tm_ffn)
    return x


def kernel(x_prompt, x_sample, ln1_g, w_in, mla_q_norm_g, w_q_up, mla_kv_norm_g, w_kv_up,
           mla_qn_g, mla_kn_g, diff_qn_g, diff_kn_g, lambda_q1, lambda_k1, lambda_q2,
           lambda_k2, diff_subln_g, w_out, ln2_g, w_gate, conv_w, conv_b, w_up, w_down):
    params = dict(ln1_g=ln1_g, w_in=w_in, mla_q_norm_g=mla_q_norm_g, w_q_up=w_q_up,
                  mla_kv_norm_g=mla_kv_norm_g, w_kv_up=w_kv_up, mla_qn_g=mla_qn_g,
                  mla_kn_g=mla_kn_g, diff_qn_g=diff_qn_g, diff_kn_g=diff_kn_g,
                  lambda_q1=lambda_q1, lambda_k1=lambda_k1, lambda_q2=lambda_q2,
                  lambda_k2=lambda_k2, diff_subln_g=diff_subln_g, w_out=w_out, ln2_g=ln2_g,
                  w_gate=w_gate, conv_w=conv_w, conv_b=conv_b, w_up=w_up, w_down=w_down)
    nb = x_prompt.shape[0]
    assert x_prompt.shape[1:] == x_sample.shape[1:]
    x = jnp.concatenate([x_prompt, x_sample], axis=0)
    y = _trunk(x, params)
    return (y[:nb], y[nb:])
```

```python
import functools
import math

import jax
import jax.numpy as jnp
import numpy as np
from jax import lax
from jax.experimental import pallas as pl
from jax.experimental.pallas import tpu as pltpu

D_MODEL = 1024
DEPTH = 4
MLA_HEADS = 8
MLA_Q_LORA = 256
MLA_KV_LORA = 128
MLA_NOPE = 64
MLA_ROPE = 32
MLA_V = 64
MLA_QK = MLA_NOPE + MLA_ROPE
MLA_THETA = 10000.0
DIFF_HEADS = 4
DIFF_DK = 64
DIFF_DV = 2 * DIFF_DK
DIFF_ROPE = DIFF_DK // 4
ROPE_THETA = 500000.0
D_FF = 2816
EPS = 1e-6

LANES = 128
MXU_DTYPE = jnp.bfloat16
VMEM_LIMIT = 56 * 1024 * 1024

PRE_TM = 512
ATT_TQ = 1024
ATT_TK = 256
FFN_TM = 512
FFN_HALO = 16
FFN_CHUNK = 256
NEG_BIG = -1e30
LOG2E = math.log2(math.e)

_NT = (((1,), (1,)), ((), ()))


def _dot(a, b):
    return jnp.dot(a, b, preferred_element_type=jnp.float32)


def _dot_nt(a, b):
    return lax.dot_general(a, b, _NT, preferred_element_type=jnp.float32)


def _rms(x, g):
    ms = jnp.mean(x * x, axis=-1, keepdims=True)
    return x * lax.rsqrt(ms + EPS) * g


def _group_rsqrt(xc, block_ones, width):
    ssq = _dot((xc * xc).astype(MXU_DTYPE), block_ones)
    return lax.rsqrt(ssq * (1.0 / width) + EPS)


def _rope_group(xg, cos, sin_a, sin_b, shift):
    return (xg * cos + pltpu.roll(xg, LANES - shift, 1) * sin_a
            + pltpu.roll(xg, shift, 1) * sin_b)


def _pre_kernel(x_ref, ln1_ref, win_ref, wdvt_ref, gcq_ref, wq_ref, gckv_ref, wk_ref, wvt_ref,
                gqn_ref, gkn_ref, gdq_ref, gdk_ref,
                cosm_ref, sma_ref, smb_ref, cosd_ref, sda_ref, sdb_ref,
                bo128_ref, bo64_ref,
                qm_ref, km_ref, vmt_ref, qd_ref, kd_ref, vdt_ref):
    tm = x_ref.shape[1]
    tk = vmt_ref.shape[3]
    x = x_ref[0]
    hb = _rms(x, ln1_ref[...]).astype(MXU_DTYPE)
    proj = _dot(hb, win_ref[...])
    c_q = proj[:, 0:256]
    c_kv = proj[:, 256:384]
    kpe = proj[:, 384:512]
    dq = proj[:, 512:1024]
    dk = proj[:, 1024:1536]

    dvt = _dot_nt(wdvt_ref[...], hb).astype(MXU_DTYPE)
    cqn = _rms(c_q, gcq_ref[...]).astype(MXU_DTYPE)
    ckvn = _rms(c_kv, gckv_ref[...]).astype(MXU_DTYPE)
    q = _dot(cqn, wq_ref[...])
    kn = _dot(ckvn, wk_ref[...])
    vt = _dot_nt(wvt_ref[...], ckvn).astype(MXU_DTYPE)
    for t in range(tm // tk):
        vmt_ref[0, t] = vt[:, t * tk:(t + 1) * tk]
        vdt_ref[0, t] = dvt[:, t * tk:(t + 1) * tk]

    bo128 = bo128_ref[...]
    bo64 = bo64_ref[...]
    cosm, sma, smb = cosm_ref[...], sma_ref[...], smb_ref[...]
    cosd, sda, sdb = cosd_ref[...], sda_ref[...], sdb_ref[...]
    gqn, gkn, gdq, gdk = gqn_ref[...], gkn_ref[...], gdq_ref[...], gdk_ref[...]
    kpe2 = jnp.concatenate([kpe, kpe], axis=1)
    q_scale = MLA_QK ** -0.5 * LOG2E
    d_scale = DIFF_DK ** -0.5 * LOG2E
    half_m = MLA_ROPE // 2
    half_d = DIFF_ROPE // 2
    lane = lax.broadcasted_iota(jnp.int32, (tm, LANES), 1)

    for c in range(MLA_HEADS // 2):
        sl = slice(c * 256, (c + 1) * 256)
        qc = q[:, sl]
        qc = qc * _group_rsqrt(qc, bo128, MLA_QK)
        kc = kn[:, sl] + kpe2
        kc = kc * _group_rsqrt(kc, bo128, MLA_QK)
        for g in range(2):
            gs = slice(g * LANES, (g + 1) * LANES)
            out = slice((2 * c + g) * LANES, (2 * c + g + 1) * LANES)
            qg = _rope_group(qc[:, gs] * gqn, cosm, sma, smb, half_m)
            qm_ref[0, :, out] = (qg * q_scale).astype(MXU_DTYPE)
            kg = _rope_group(kc[:, gs] * gkn, cosm, sma, smb, half_m)
            km_ref[0, :, out] = kg.astype(MXU_DTYPE)

    for c in range(DIFF_HEADS // 2):
        sl = slice(c * 256, (c + 1) * 256)
        qc = dq[:, sl]
        qc = qc * _group_rsqrt(qc, bo64, DIFF_DK)
        kc = dk[:, sl]
        kc = kc * _group_rsqrt(kc, bo64, DIFF_DK)
        for g in range(2):
            h = 2 * c + g
            gs = slice(g * LANES, (g + 1) * LANES)
            qg = _rope_group(qc[:, gs] * gdq, cosd, sda, sdb, half_d) * d_scale
            zero = jnp.zeros_like(qg)
            qd_ref[0, :, (2 * h) * LANES:(2 * h + 1) * LANES] = (
                jnp.where(lane < DIFF_DK, qg, zero).astype(MXU_DTYPE))
            qd_ref[0, :, (2 * h + 1) * LANES:(2 * h + 2) * LANES] = (
                jnp.where(lane >= DIFF_DK, qg, zero).astype(MXU_DTYPE))
            kg = _rope_group(kc[:, gs] * gdk, cosd, sda, sdb, half_d)
            kd_ref[0, :, h * LANES:(h + 1) * LANES] = kg.astype(MXU_DTYPE)


def _pre_call(x, lp, tabs, tm, tk):
    b, s, d = x.shape
    nt = s // tm
    tpb = tm // tk

    def full(a):
        return pl.BlockSpec(a.shape, lambda bi, ti: (0,) * a.ndim)

    def tab_spec():
        return pl.BlockSpec((tm, LANES), lambda bi, ti: (ti, 0))

    weights = [lp['ln1'], lp['w_in'], lp['w_dvt'], lp['g_cq'], lp['w_q'], lp['g_ckv'], lp['w_k'],
               lp['w_vt'], lp['g_qn'], lp['g_kn'], lp['g_dq'], lp['g_dk']]
    consts = [tabs['bo128'], tabs['bo64']]
    rope = [tabs['cosm'], tabs['sma'], tabs['smb'], tabs['cosd'], tabs['sda'], tabs['sdb']]
    in_specs = ([pl.BlockSpec((1, tm, d), lambda bi, ti: (bi, ti, 0))]
                + [full(a) for a in weights] + [tab_spec() for _ in rope] + [full(a) for a in consts])
    tok = lambda w: pl.BlockSpec((1, tm, w), lambda bi, ti: (bi, ti, 0))
    vts = pl.BlockSpec((1, tpb, 512, tk), lambda bi, ti: (bi, ti, 0, 0))
    out_shape = [
        jax.ShapeDtypeStruct((b, s, 1024), MXU_DTYPE),
        jax.ShapeDtypeStruct((b, s, 1024), MXU_DTYPE),
        jax.ShapeDtypeStruct((b, s // tk, 512, tk), MXU_DTYPE),
        jax.ShapeDtypeStruct((b, s, 1024), MXU_DTYPE),
        jax.ShapeDtypeStruct((b, s, 512), MXU_DTYPE),
        jax.ShapeDtypeStruct((b, s // tk, 512, tk), MXU_DTYPE),
    ]
    out_specs = [tok(1024), tok(1024), vts, tok(1024), tok(512), vts]
    return pl.pallas_call(
        _pre_kernel,
        grid=(b, nt),
        in_specs=in_specs,
        out_specs=out_specs,
        out_shape=out_shape,
        compiler_params=pltpu.CompilerParams(
            dimension_semantics=("arbitrary", "arbitrary"), vmem_limit_bytes=VMEM_LIMIT),
        name="proj",
    )(x, *weights, *rope, *consts)


def _flash_pipelined(nk, n_streams, qk_fn, pv_fn, s_scr, p_scr, tq, dvp):
    for st in range(n_streams):
        s_scr[st] = qk_fn(0, st)
        p_scr[st] = jnp.zeros(p_scr.shape[1:], p_scr.dtype)

    def body(j, carry):
        jn = jnp.minimum(j + 1, nk - 1)
        jp = jnp.maximum(j - 1, 0)
        out = []
        for st in range(n_streams):
            m, alpha, acc = carry[st]
            s_cur = s_scr[st]
            acc = acc * alpha + pv_fn(jp, st, p_scr[st])
            s_nxt = qk_fn(jn, st)
            m_new = jnp.maximum(m, jnp.max(s_cur, axis=0, keepdims=True))
            p_scr[st] = jnp.exp2(s_cur - m_new).astype(p_scr.dtype)
            s_scr[st] = s_nxt
            out.append((m_new, jnp.exp2(m - m_new), acc))
        return tuple(out)

    init = tuple((jnp.full((1, tq), NEG_BIG, jnp.float32), jnp.ones((1, tq), jnp.float32),
                  jnp.zeros((dvp, tq), jnp.float32)) for _ in range(n_streams))
    res = lax.fori_loop(0, nk, body, init)
    return [res[st][2] * res[st][1] + pv_fn(nk - 1, st, p_scr[st]) for st in range(n_streams)]


def _mla_attn_kernel(q_ref, k_ref, vt_ref, o_ref, s_scr, p_scr, *, tq, tk):
    s_len = q_ref.shape[1]
    nq, nk = s_len // tq, s_len // tk
    ones = jnp.ones((16, tk), MXU_DTYPE)
    dv = MLA_V

    def qbody(qi, c):
        qoff = pl.multiple_of(qi * tq, tq)

        def qk(j, h):
            k = k_ref[0, pl.ds(pl.multiple_of(j * tk, tk), tk), h * LANES:(h + 1) * LANES]
            return _dot_nt(k, q_ref[0, pl.ds(qoff, tq), h * LANES:(h + 1) * LANES])

        def pv(j, h, p):
            vt = jnp.concatenate([vt_ref[0, j, h * dv:(h + 1) * dv, :], ones], axis=0)
            return _dot(vt, p)

        accs = _flash_pipelined(nk, 2, qk, pv, s_scr, p_scr, tq, dv + 16)
        o = jnp.concatenate([a[:dv] / a[dv:dv + 1] for a in accs], axis=0)
        o_ref[0, pl.ds(qoff, tq), :] = o.T.astype(o_ref.dtype)
        return c

    lax.fori_loop(0, nq, qbody, 0)


def _mla_attn_call(qm, km, vmt, tq, tk):
    b, s, _ = qm.shape
    kern = functools.partial(_mla_attn_kernel, tq=tq, tk=tk)
    return pl.pallas_call(
        kern,
        grid=(b, MLA_HEADS // 2),
        in_specs=[
            pl.BlockSpec((1, s, 2 * LANES), lambda bi, hp: (bi, 0, hp)),
            pl.BlockSpec((1, s, 2 * LANES), lambda bi, hp: (bi, 0, hp)),
            pl.BlockSpec((1, s // tk, 2 * MLA_V, tk), lambda bi, hp: (bi, 0, hp, 0)),
        ],
        out_specs=pl.BlockSpec((1, s, 2 * MLA_V), lambda bi, hp: (bi, 0, hp)),
        out_shape=jax.ShapeDtypeStruct((b, s, MLA_HEADS * MLA_V), MXU_DTYPE),
        scratch_shapes=[pltpu.VMEM((2, tk, tq), jnp.float32), pltpu.VMEM((2, tk, tq), MXU_DTYPE)],
        compiler_params=pltpu.CompilerParams(
            dimension_semantics=("arbitrary", "arbitrary"), vmem_limit_bytes=VMEM_LIMIT),
        name="mla_attn",
    )(qm, km, vmt)


def _diff_attn_kernel(q_ref, k_ref, vt_ref, lq1_ref, lk1_ref, lq2_ref, lk2_ref, lam0_ref, g_ref,
                      o_ref, s_scr, p_scr, *, tq, tk):
    s_len = q_ref.shape[1]
    nq, nk = s_len // tq, s_len // tk
    ones = jnp.ones((16, tk), MXU_DTYPE)
    dv = DIFF_DV
    lam0 = lam0_ref[...]
    lam = (jnp.exp(jnp.sum(lq1_ref[...] * lk1_ref[...], axis=-1, keepdims=True))
           - jnp.exp(jnp.sum(lq2_ref[...] * lk2_ref[...], axis=-1, keepdims=True)) + lam0)
    gain = g_ref[...] * (1.0 - lam0)

    def qbody(qi, c):
        qoff = pl.multiple_of(qi * tq, tq)

        def qk(j, mi):
            k = k_ref[0, pl.ds(pl.multiple_of(j * tk, tk), tk), :]
            return _dot_nt(k, q_ref[0, pl.ds(qoff, tq), mi * LANES:(mi + 1) * LANES])

        def pv(j, mi, p):
            return _dot(jnp.concatenate([vt_ref[0, j], ones], axis=0), p)

        a1, a2 = _flash_pipelined(nk, 2, qk, pv, s_scr, p_scr, tq, dv + 16)
        o = a1[:dv] / a1[dv:dv + 1] - lam * (a2[:dv] / a2[dv:dv + 1])
        ms = jnp.mean(o * o, axis=0, keepdims=True)
        y = o * lax.rsqrt(ms + EPS) * gain
        o_ref[0, pl.ds(qoff, tq), :] = y.T.astype(o_ref.dtype)
        return c

    lax.fori_loop(0, nq, qbody, 0)


def _diff_attn_call(qd, kd, vdt, lp, tq, tk):
    b, s, _ = qd.shape
    kern = functools.partial(_diff_attn_kernel, tq=tq, tk=tk)
    small = [lp['lq1'], lp['lk1'], lp['lq2'], lp['lk2'], lp['lam0'], lp['g_sub']]
    return pl.pallas_call(
        kern,
        grid=(b, DIFF_HEADS),
        in_specs=[
            pl.BlockSpec((1, s, 2 * LANES), lambda bi, h: (bi, 0, h)),
            pl.BlockSpec((1, s, LANES), lambda bi, h: (bi, 0, h)),
            pl.BlockSpec((1, s // tk, DIFF_DV, tk), lambda bi, h: (bi, 0, h, 0)),
        ] + [pl.BlockSpec(a.shape, lambda bi, h: (0, 0)) for a in small],
        out_specs=pl.BlockSpec((1, s, DIFF_DV), lambda bi, h: (bi, 0, h)),
        out_shape=jax.ShapeDtypeStruct((b, s, DIFF_HEADS * DIFF_DV), MXU_DTYPE),
        scratch_shapes=[pltpu.VMEM((2, tk, tq), jnp.float32), pltpu.VMEM((2, tk, tq), MXU_DTYPE)],
        compiler_params=pltpu.CompilerParams(
            dimension_semantics=("arbitrary", "arbitrary"), vmem_limit_bytes=VMEM_LIMIT),
        name="diff_attn",
    )(qd, kd, vdt, *small)


def _ffn_kernel(x_ref, xp_ref, xn_ref, am_ref, amp_ref, amn_ref, ad_ref, adp_ref, adn_ref,
                wom_ref, wod_ref, ln2_ref, wg_ref, wu_ref, cw_ref, wd_ref,
                o_ref, h_scr, g_scr):
    tm = x_ref.shape[1]
    halo = xp_ref.shape[1]
    ti = pl.program_id(1)
    nt = pl.num_programs(1)
    nchunk = wg_ref.shape[0]

    x_ext = jnp.concatenate([xp_ref[0], x_ref[0], xn_ref[0]], axis=0)
    am_ext = jnp.concatenate([amp_ref[0], am_ref[0], amn_ref[0]], axis=0)
    ad_ext = jnp.concatenate([adp_ref[0], ad_ref[0], adn_ref[0]], axis=0)
    xmid = x_ext + _dot(am_ext, wom_ref[...]) + _dot(ad_ext, wod_ref[...])
    h_scr[...] = _rms(xmid, ln2_ref[...]).astype(h_scr.dtype)
    o_ref[0] = xmid[halo:halo + tm]

    row = lax.broadcasted_iota(jnp.int32, (tm + 2 * halo, 1), 0)
    valid = jnp.logical_and(jnp.logical_or(row >= halo, ti > 0),
                            jnp.logical_or(row < halo + tm, ti < nt - 1))

    def chunk(c, carry):
        h_ext = h_scr[...]
        g = _dot(h_ext, wg_ref[c])
        g_scr[...] = jnp.where(valid, g, 0.0)
        u = _dot(h_ext[halo:halo + tm], wu_ref[c])
        cw = cw_ref[c]
        conv = (g_scr[pl.ds(halo - 1, tm), :] * cw[0:1] + g_scr[pl.ds(halo, tm), :] * cw[1:2]
                + g_scr[pl.ds(halo + 1, tm), :] * cw[2:3] + cw[3:4])
        a = (conv * jax.nn.sigmoid(conv) * u).astype(h_scr.dtype)
        o_ref[0] += _dot(a, wd_ref[c])
        return carry

    lax.fori_loop(0, nchunk, chunk, 0)


def _ffn_call(x, am, ad, lp, tm):
    b, s, d = x.shape
    nt = s // tm
    halo = FFN_HALO
    r = tm // halo
    nhb = s // halo

    def main(w):
        return pl.BlockSpec((1, tm, w), lambda bi, ti: (bi, ti, 0))

    def prev(w):
        return pl.BlockSpec((1, halo, w), lambda bi, ti: (bi, jnp.maximum(ti * r - 1, 0), 0))

    def nxt(w):
        return pl.BlockSpec((1, halo, w), lambda bi, ti: (bi, jnp.minimum((ti + 1) * r, nhb - 1), 0))

    def resident(a):
        return pl.BlockSpec(a.shape, lambda bi, ti: (0,) * a.ndim, pipeline_mode=pl.Buffered(1))

    weights = [lp['wo_m'], lp['wo_d'], lp['ln2'], lp['w_gate'], lp['w_up'], lp['conv'], lp['w_down']]
    wa = am.shape[-1]
    in_specs = ([main(d), prev(d), nxt(d), main(wa), prev(wa), nxt(wa), main(wa), prev(wa), nxt(wa)]
                + [resident(a) for a in weights])
    return pl.pallas_call(
        _ffn_kernel,
        grid=(b, nt),
        in_specs=in_specs,
        out_specs=main(d),
        out_shape=jax.ShapeDtypeStruct((b, s, d), jnp.float32),
        scratch_shapes=[pltpu.VMEM((tm + 2 * halo, d), MXU_DTYPE),
                        pltpu.VMEM((tm + 2 * halo, FFN_CHUNK), jnp.float32)],
        compiler_params=pltpu.CompilerParams(
            dimension_semantics=("arbitrary", "arbitrary"), vmem_limit_bytes=VMEM_LIMIT),
        name="ffn",
    )(x, x, x, am, am, am, ad, ad, ad, *weights)


def _rope_tables(s, half, theta, lane_starts):
    d = 2 * half
    freqs = theta ** (-jnp.arange(half, dtype=jnp.float32) * 2.0 / d)
    ang = jnp.arange(s, dtype=jnp.float32)[:, None] * freqs[None, :]
    cos, sin = jnp.cos(ang), jnp.sin(ang)
    cos_t = jnp.ones((s, LANES), jnp.float32)
    sa = jnp.zeros((s, LANES), jnp.float32)
    sb = jnp.zeros((s, LANES), jnp.float32)
    for st in lane_starts:
        cos_t = cos_t.at[:, st:st + half].set(cos).at[:, st + half:st + d].set(cos)
        sa = sa.at[:, st:st + half].set(-sin)
        sb = sb.at[:, st + half:st + d].set(sin)
    return cos_t, sa, sb


def _block_ones(width):
    idx = np.arange(256) // width
    return jnp.asarray(idx[:, None] == idx[None, :], MXU_DTYPE)


def _tables(s):
    cosm, sma, smb = _rope_tables(s, MLA_ROPE // 2, MLA_THETA, (MLA_NOPE,))
    cosd, sda, sdb = _rope_tables(s, DIFF_ROPE // 2, ROPE_THETA, (0, DIFF_DK))
    return dict(cosm=cosm, sma=sma, smb=smb, cosd=cosd, sda=sda, sdb=sdb,
                bo128=_block_ones(LANES), bo64=_block_ones(DIFF_DK))


def _layer_params(l, p):
    f32 = jnp.float32
    cd = MXU_DTYPE
    w_in = p['w_in'][l]
    o_cq, o_ckv, o_kpe = 0, MLA_Q_LORA, MLA_Q_LORA + MLA_KV_LORA
    o_dq = o_kpe + MLA_ROPE
    o_dk = o_dq + DIFF_HEADS * 2 * DIFF_DK
    o_dv = o_dk + DIFF_HEADS * 2 * DIFF_DK
    kpe_pad = jnp.zeros((D_MODEL, LANES), f32).at[:, MLA_NOPE:MLA_QK].set(w_in[:, o_kpe:o_dq])
    w_in_main = jnp.concatenate([w_in[:, o_cq:o_kpe], kpe_pad, w_in[:, o_dq:o_dv]], axis=1)
    w_q = p['w_q_up'][l].reshape(MLA_Q_LORA, MLA_HEADS, MLA_QK)
    w_q = jnp.pad(w_q, ((0, 0), (0, 0), (0, LANES - MLA_QK))).reshape(MLA_Q_LORA, MLA_HEADS * LANES)
    w_kv = p['w_kv_up'][l].reshape(MLA_KV_LORA, MLA_HEADS, MLA_NOPE + MLA_V)
    w_k = jnp.pad(w_kv[:, :, :MLA_NOPE], ((0, 0), (0, 0), (0, LANES - MLA_NOPE)))
    w_k = w_k.reshape(MLA_KV_LORA, MLA_HEADS * LANES)
    w_vt = w_kv[:, :, MLA_NOPE:].reshape(MLA_KV_LORA, MLA_HEADS * MLA_V).T
    nchunk = D_FF // FFN_CHUNK
    conv = jnp.concatenate([p['conv_w'][l], p['conv_b'][l][None, :],
                            jnp.zeros((4, D_FF), f32)], axis=0)
    lam0 = 0.8 - 0.6 * math.exp(-0.3 * l)
    row = lambda v: v.astype(f32)[None, :]
    w_out = p['w_out'][l]
    return dict(
        ln1=row(p['ln1_g'][l]),
        w_in=w_in_main.astype(cd),
        w_dvt=w_in[:, o_dv:].T.astype(cd),
        g_cq=row(p['mla_q_norm_g'][l]),
        w_q=w_q.astype(cd),
        g_ckv=row(p['mla_kv_norm_g'][l]),
        w_k=w_k.astype(cd),
        w_vt=w_vt.astype(cd),
        g_qn=row(jnp.pad(p['mla_qn_g'][l], (0, LANES - MLA_QK))),
        g_kn=row(jnp.pad(p['mla_kn_g'][l], (0, LANES - MLA_QK))),
        g_dq=row(jnp.tile(p['diff_qn_g'][l], 2)),
        g_dk=row(jnp.tile(p['diff_kn_g'][l], 2)),
        lq1=row(p['lambda_q1'][l]), lk1=row(p['lambda_k1'][l]),
        lq2=row(p['lambda_q2'][l]), lk2=row(p['lambda_k2'][l]),
        lam0=jnp.full((1, 1), lam0, f32),
        g_sub=p['diff_subln_g'][l].astype(f32)[:, None],
        wo_m=w_out[:MLA_HEADS * MLA_V].astype(cd),
        wo_d=w_out[MLA_HEADS * MLA_V:].astype(cd),
        ln2=row(p['ln2_g'][l]),
        w_gate=p['w_gate'][l].reshape(D_MODEL, nchunk, FFN_CHUNK).transpose(1, 0, 2).astype(cd),
        w_up=p['w_up'][l].reshape(D_MODEL, nchunk, FFN_CHUNK).transpose(1, 0, 2).astype(cd),
        conv=conv.reshape(8, nchunk, FFN_CHUNK).transpose(1, 0, 2),
        w_down=p['w_down'][l].reshape(nchunk, FFN_CHUNK, D_MODEL).astype(cd),
    )


def _trunk(x, params):
    b, s, _ = x.shape
    tq = min(ATT_TQ, s)
    tk = min(ATT_TK, s)
    tm_pre = min(PRE_TM, s)
    tm_ffn = min(FFN_TM, s)
    tabs = _tables(s)
    for l in range(DEPTH):
        lp = _layer_params(l, params)
        qm, km, vmt, qd, kd, vdt = _pre_call(x, lp, tabs, tm_pre, tk)
        am = _mla_attn_call(qm, km, vmt, tq, tk)
        ad = _diff_attn_call(qd, kd, vdt, lp, tq, tk)
        x = _ffn_call(x, am, ad, lp, tm_ffn)
    return x


def kernel(x_prompt, x_sample, ln1_g, w_in, mla_q_norm_g, w_q_up, mla_kv_norm_g, w_kv_up,
           mla_qn_g, mla_kn_g, diff_qn_g, diff_kn_g, lambda_q1, lambda_k1, lambda_q2,
           lambda_k2, diff_subln_g, w_out, ln2_g, w_gate, conv_w, conv_b, w_up, w_down):
    params = dict(ln1_g=ln1_g, w_in=w_in, mla_q_norm_g=mla_q_norm_g, w_q_up=w_q_up,
                  mla_kv_norm_g=mla_kv_norm_g, w_kv_up=w_kv_up, mla_qn_g=mla_qn_g,
                  mla_kn_g=mla_kn_g, diff_qn_g=diff_qn_g, diff_kn_g=diff_kn_g,
                  lambda_q1=lambda_q1, lambda_k1=lambda_k1, lambda_q2=lambda_q2,
                  lambda_k2=lambda_k2, diff_subln_g=diff_subln_g, w_out=w_out, ln2_g=ln2_g,
                  w_gate=w_gate, conv_w=conv_w, conv_b=conv_b, w_up=w_up, w_down=w_down)
    nb = x_prompt.shape[0]
    assert x_prompt.shape[1:] == x_sample.shape[1:]
    x = jnp.concatenate([x_prompt, x_sample], axis=0)
    y = _trunk(x, params)
    return (y[:nb], y[nb:])
```

```python
import functools
import math

import jax
import jax.numpy as jnp
import numpy as np
from jax import lax
from jax.experimental import pallas as pl
from jax.experimental.pallas import tpu as pltpu

D_MODEL = 1024
DEPTH = 4
MLA_HEADS = 8
MLA_Q_LORA = 256
MLA_KV_LORA = 128
MLA_NOPE = 64
MLA_ROPE = 32
MLA_V = 64
MLA_QK = MLA_NOPE + MLA_ROPE
MLA_THETA = 10000.0
DIFF_HEADS = 4
DIFF_DK = 64
DIFF_DV = 2 * DIFF_DK
DIFF_ROPE = DIFF_DK // 4
ROPE_THETA = 500000.0
D_FF = 2816
EPS = 1e-6

LANES = 128
MXU_DTYPE = jnp.bfloat16
VMEM_LIMIT = 56 * 1024 * 1024

PRE_TM = 512
ATT_TQ = 1024
ATT_TK = 1024
VT_TILE = 512
ATT_SUB = 256
FFN_TM = 512
FFN_HALO = 16
FFN_CHUNK = 256
NEG_BIG = -1e30
LOG2E = math.log2(math.e)
SCORE_BOUND_LOG2 = 60.0
NORM_SLACK = 1.05

_NT = (((1,), (1,)), ((), ()))


def _dot(a, b):
    return jnp.dot(a, b, preferred_element_type=jnp.float32)


def _dot_nt(a, b):
    return lax.dot_general(a, b, _NT, preferred_element_type=jnp.float32)


def _rms(x, g):
    ms = jnp.mean(x * x, axis=-1, keepdims=True)
    return x * lax.rsqrt(ms + EPS) * g


def _group_rsqrt(xc, block_ones, width):
    ssq = _dot((xc * xc).astype(MXU_DTYPE), block_ones)
    return lax.rsqrt(ssq * (1.0 / width) + EPS)


def _rope_group(xg, cos, sin_a, sin_b, shift):
    return (xg * cos + pltpu.roll(xg, LANES - shift, 1) * sin_a
            + pltpu.roll(xg, shift, 1) * sin_b)


def _pre_kernel(x_ref, ln1_ref, win_ref, wdvt_ref, gcq_ref, wq_ref, gckv_ref, wk_ref, wvt_ref,
                gqn_ref, gkn_ref, gdq_ref, gdk_ref,
                cosm_ref, sma_ref, smb_ref, cosd_ref, sda_ref, sdb_ref,
                bo128_ref, bo64_ref,
                qm_ref, km_ref, vmt_ref, qd_ref, kd_ref, vdt_ref):
    tm = x_ref.shape[1]
    tk = vmt_ref.shape[3]
    x = x_ref[0]
    hb = _rms(x, ln1_ref[...]).astype(MXU_DTYPE)
    proj = _dot(hb, win_ref[...])
    c_q = proj[:, 0:256]
    c_kv = proj[:, 256:384]
    kpe = proj[:, 384:512]
    dq = proj[:, 512:1024]
    dk = proj[:, 1024:1536]

    dvt = _dot_nt(wdvt_ref[...], hb).astype(MXU_DTYPE)
    cqn = _rms(c_q, gcq_ref[...]).astype(MXU_DTYPE)
    ckvn = _rms(c_kv, gckv_ref[...]).astype(MXU_DTYPE)
    q = _dot(cqn, wq_ref[...])
    kn = _dot(ckvn, wk_ref[...])
    vt = _dot_nt(wvt_ref[...], ckvn).astype(MXU_DTYPE)
    for t in range(tm // tk):
        vmt_ref[0, t] = vt[:, t * tk:(t + 1) * tk]
        vdt_ref[0, t] = dvt[:, t * tk:(t + 1) * tk]

    bo128 = bo128_ref[...]
    bo64 = bo64_ref[...]
    cosm, sma, smb = cosm_ref[...], sma_ref[...], smb_ref[...]
    cosd, sda, sdb = cosd_ref[...], sda_ref[...], sdb_ref[...]
    gqn, gkn, gdq, gdk = gqn_ref[...], gkn_ref[...], gdq_ref[...], gdk_ref[...]
    kpe2 = jnp.concatenate([kpe, kpe], axis=1)
    q_scale = MLA_QK ** -0.5 * LOG2E
    d_scale = DIFF_DK ** -0.5 * LOG2E
    half_m = MLA_ROPE // 2
    half_d = DIFF_ROPE // 2
    lane = lax.broadcasted_iota(jnp.int32, (tm, LANES), 1)

    for c in range(MLA_HEADS // 2):
        sl = slice(c * 256, (c + 1) * 256)
        qc = q[:, sl]
        qc = qc * _group_rsqrt(qc, bo128, MLA_QK)
        kc = kn[:, sl] + kpe2
        kc = kc * _group_rsqrt(kc, bo128, MLA_QK)
        for g in range(2):
            gs = slice(g * LANES, (g + 1) * LANES)
            out = slice((2 * c + g) * LANES, (2 * c + g + 1) * LANES)
            qg = _rope_group(qc[:, gs] * gqn, cosm, sma, smb, half_m)
            qm_ref[0, :, out] = (qg * q_scale).astype(MXU_DTYPE)
            kg = _rope_group(kc[:, gs] * gkn, cosm, sma, smb, half_m)
            km_ref[0, :, out] = kg.astype(MXU_DTYPE)

    for c in range(DIFF_HEADS // 2):
        sl = slice(c * 256, (c + 1) * 256)
        qc = dq[:, sl]
        qc = qc * _group_rsqrt(qc, bo64, DIFF_DK)
        kc = dk[:, sl]
        kc = kc * _group_rsqrt(kc, bo64, DIFF_DK)
        for g in range(2):
            h = 2 * c + g
            gs = slice(g * LANES, (g + 1) * LANES)
            qg = _rope_group(qc[:, gs] * gdq, cosd, sda, sdb, half_d) * d_scale
            zero = jnp.zeros_like(qg)
            qd_ref[0, :, (2 * h) * LANES:(2 * h + 1) * LANES] = (
                jnp.where(lane < DIFF_DK, qg, zero).astype(MXU_DTYPE))
            qd_ref[0, :, (2 * h + 1) * LANES:(2 * h + 2) * LANES] = (
                jnp.where(lane >= DIFF_DK, qg, zero).astype(MXU_DTYPE))
            kg = _rope_group(kc[:, gs] * gdk, cosd, sda, sdb, half_d)
            kd_ref[0, :, h * LANES:(h + 1) * LANES] = kg.astype(MXU_DTYPE)


def _pre_call(x, lp, tabs, tm, tk):
    b, s, d = x.shape
    nt = s // tm
    tpb = tm // tk

    def full(a):
        return pl.BlockSpec(a.shape, lambda bi, ti: (0,) * a.ndim)

    def tab_spec():
        return pl.BlockSpec((tm, LANES), lambda bi, ti: (ti, 0))

    weights = [lp['ln1'], lp['w_in'], lp['w_dvt'], lp['g_cq'], lp['w_q'], lp['g_ckv'], lp['w_k'],
               lp['w_vt'], lp['g_qn'], lp['g_kn'], lp['g_dq'], lp['g_dk']]
    consts = [tabs['bo128'], tabs['bo64']]
    rope = [tabs['cosm'], tabs['sma'], tabs['smb'], tabs['cosd'], tabs['sda'], tabs['sdb']]
    in_specs = ([pl.BlockSpec((1, tm, d), lambda bi, ti: (bi, ti, 0))]
                + [full(a) for a in weights] + [tab_spec() for _ in rope] + [full(a) for a in consts])
    tok = lambda w: pl.BlockSpec((1, tm, w), lambda bi, ti: (bi, ti, 0))
    vts = pl.BlockSpec((1, tpb, 512, tk), lambda bi, ti: (bi, ti, 0, 0))
    out_shape = [
        jax.ShapeDtypeStruct((b, s, 1024), MXU_DTYPE),
        jax.ShapeDtypeStruct((b, s, 1024), MXU_DTYPE),
        jax.ShapeDtypeStruct((b, s // tk, 512, tk), MXU_DTYPE),
        jax.ShapeDtypeStruct((b, s, 1024), MXU_DTYPE),
        jax.ShapeDtypeStruct((b, s, 512), MXU_DTYPE),
        jax.ShapeDtypeStruct((b, s // tk, 512, tk), MXU_DTYPE),
    ]
    out_specs = [tok(1024), tok(1024), vts, tok(1024), tok(512), vts]
    return pl.pallas_call(
        _pre_kernel,
        grid=(b, nt),
        in_specs=in_specs,
        out_specs=out_specs,
        out_shape=out_shape,
        compiler_params=pltpu.CompilerParams(
            dimension_semantics=("arbitrary", "arbitrary"), vmem_limit_bytes=VMEM_LIMIT),
        name="proj",
    )(x, *weights, *rope, *consts)


def _flash_pipelined(nk, n_streams, qk_fn, pv_fn, s_scr, p_scr, tq, dvp):
    for st in range(n_streams):
        s_scr[st] = qk_fn(0, st)
        p_scr[st] = jnp.zeros(p_scr.shape[1:], p_scr.dtype)

    def body(j, carry):
        jn = jnp.minimum(j + 1, nk - 1)
        jp = jnp.maximum(j - 1, 0)
        out = []
        for st in range(n_streams):
            m, alpha, acc = carry[st]
            s_cur = s_scr[st]
            acc = acc * alpha + pv_fn(jp, st, p_scr[st])
            s_nxt = qk_fn(jn, st)
            m_new = jnp.maximum(m, jnp.max(s_cur, axis=0, keepdims=True))
            p_scr[st] = jnp.exp2(s_cur - m_new).astype(p_scr.dtype)
            s_scr[st] = s_nxt
            out.append((m_new, jnp.exp2(m - m_new), acc))
        return tuple(out)

    init = tuple((jnp.full((1, tq), NEG_BIG, jnp.float32), jnp.ones((1, tq), jnp.float32),
                  jnp.zeros((dvp, tq), jnp.float32)) for _ in range(n_streams))
    res = lax.fori_loop(0, nk, body, init)
    return [res[st][2] * res[st][1] + pv_fn(nk - 1, st, p_scr[st]) for st in range(n_streams)]


def _flash_bounded(nq, nk, n_streams, qk_fn, pv_fn, finish_fn, s_scr, p_scr, acc_scr):
    total = nq * nk
    tq = s_scr.shape[2]
    for st in range(n_streams):
        s_scr[st] = jnp.full(s_scr.shape[1:], NEG_BIG, s_scr.dtype)
        p_scr[st] = jnp.zeros(p_scr.shape[1:], p_scr.dtype)
        acc_scr[st] = jnp.zeros(acc_scr.shape[1:], acc_scr.dtype)

    def body(t, c):
        ta = jnp.minimum(t, total - 1)
        tc = jnp.maximum(t - 2, 0)
        jc = tc % nk
        for sub in range(tq // ATT_SUB):
            cs = slice(sub * ATT_SUB, (sub + 1) * ATT_SUB)
            for st in range(n_streams):
                acc_scr[st, :, cs] += pv_fn(jc, st, p_scr[st, :, cs])
                p_scr[st, :, cs] = jnp.exp2(s_scr[st, :, cs]).astype(p_scr.dtype)
                s_scr[st, :, cs] = qk_fn(ta // nk, ta % nk, st, sub)

        @pl.when(jnp.logical_and(t >= 2, jc == nk - 1))
        def _():
            finish_fn(tc // nk, [acc_scr[st] for st in range(n_streams)])
            for st in range(n_streams):
                acc_scr[st] = jnp.zeros(acc_scr.shape[1:], acc_scr.dtype)

        return c

    lax.fori_loop(0, total + 2, body, 0)


def _q_rows(qi, sub, tq):
    if sub is None:
        return pl.ds(pl.multiple_of(qi * tq, tq), tq)
    return pl.ds(pl.multiple_of(qi * tq + sub * ATT_SUB, ATT_SUB), ATT_SUB)


def _vt_tile(vt_ref, j, tk, rows):
    tv = vt_ref.shape[3]
    n = tk // tv
    parts = [vt_ref[0, j * n + i, rows, :] for i in range(n)]
    return parts[0] if n == 1 else jnp.concatenate(parts, axis=1)


def _flash_scratch(tq, tk, dvp, n_streams=2):
    return [pltpu.VMEM((n_streams, tk, tq), jnp.float32), pltpu.VMEM((n_streams, tk, tq), MXU_DTYPE),
            pltpu.VMEM((n_streams, dvp, tq), jnp.float32)]


def _flash_either(bounded_ref, nq, nk, n_streams, qk_fn, pv_fn, finish_fn, scratch):
    s_scr, p_scr, acc_scr = scratch
    tq = s_scr.shape[2]
    dvp = acc_scr.shape[1]

    @pl.when(bounded_ref[0] != 0)
    def _():
        _flash_bounded(nq, nk, n_streams, qk_fn, pv_fn, finish_fn, s_scr, p_scr, acc_scr)

    @pl.when(bounded_ref[0] == 0)
    def _():
        def qbody(qi, c):
            accs = _flash_pipelined(nk, n_streams, lambda j, st: qk_fn(qi, j, st, None), pv_fn,
                                    s_scr, p_scr, tq, dvp)
            finish_fn(qi, accs)
            return c

        lax.fori_loop(0, nq, qbody, 0)


def _mla_attn_kernel(bounded_ref, q_ref, k_ref, vt_ref, o_ref, *scratch, tq, tk):
    s_len = q_ref.shape[1]
    nq, nk = s_len // tq, s_len // tk
    ones = jnp.ones((16, tk), MXU_DTYPE)
    dv = MLA_V

    def qk(qi, j, h, sub):
        k = k_ref[0, pl.ds(pl.multiple_of(j * tk, tk), tk), h * LANES:(h + 1) * LANES]
        return _dot_nt(k, q_ref[0, _q_rows(qi, sub, tq), h * LANES:(h + 1) * LANES])

    def pv(j, h, p):
        vt = jnp.concatenate([_vt_tile(vt_ref, j, tk, slice(h * dv, (h + 1) * dv)), ones], axis=0)
        return _dot(vt, p)

    def finish(qi, accs):
        o = jnp.concatenate([a[:dv] / a[dv:dv + 1] for a in accs], axis=0)
        o_ref[0, pl.ds(pl.multiple_of(qi * tq, tq), tq), :] = o.T.astype(o_ref.dtype)

    _flash_either(bounded_ref, nq, nk, 2, qk, pv, finish, scratch)


def _mla_attn_call(bounded, qm, km, vmt, tq, tk):
    b, s, _ = qm.shape
    kern = functools.partial(_mla_attn_kernel, tq=tq, tk=tk)
    return pl.pallas_call(
        kern,
        grid=(b, MLA_HEADS // 2),
        in_specs=[
            pl.BlockSpec(memory_space=pltpu.SMEM),
            pl.BlockSpec((1, s, 2 * LANES), lambda bi, hp: (bi, 0, hp)),
            pl.BlockSpec((1, s, 2 * LANES), lambda bi, hp: (bi, 0, hp)),
            pl.BlockSpec((1, vmt.shape[1], 2 * MLA_V, vmt.shape[3]), lambda bi, hp: (bi, 0, hp, 0)),
        ],
        out_specs=pl.BlockSpec((1, s, 2 * MLA_V), lambda bi, hp: (bi, 0, hp)),
        out_shape=jax.ShapeDtypeStruct((b, s, MLA_HEADS * MLA_V), MXU_DTYPE),
        scratch_shapes=_flash_scratch(tq, tk, MLA_V + 16),
        compiler_params=pltpu.CompilerParams(
            dimension_semantics=("arbitrary", "arbitrary"), vmem_limit_bytes=VMEM_LIMIT),
        name="mla_attn",
    )(bounded, qm, km, vmt)


def _diff_attn_kernel(bounded_ref, q_ref, k_ref, vt_ref, lq1_ref, lk1_ref, lq2_ref, lk2_ref,
                      lam0_ref, g_ref, o_ref, *scratch, tq, tk):
    s_len = q_ref.shape[1]
    nq, nk = s_len // tq, s_len // tk
    ones = jnp.ones((16, tk), MXU_DTYPE)
    dv = DIFF_DV
    lam0 = lam0_ref[...]
    lam = (jnp.exp(jnp.sum(lq1_ref[...] * lk1_ref[...], axis=-1, keepdims=True))
           - jnp.exp(jnp.sum(lq2_ref[...] * lk2_ref[...], axis=-1, keepdims=True)) + lam0)
    gain = g_ref[...] * (1.0 - lam0)

    def qk(qi, j, mi, sub):
        k = k_ref[0, pl.ds(pl.multiple_of(j * tk, tk), tk), :]
        return _dot_nt(k, q_ref[0, _q_rows(qi, sub, tq), mi * LANES:(mi + 1) * LANES])

    def pv(j, mi, p):
        return _dot(jnp.concatenate([_vt_tile(vt_ref, j, tk, slice(None)), ones], axis=0), p)

    def finish(qi, accs):
        a1, a2 = accs
        o = a1[:dv] / a1[dv:dv + 1] - lam * (a2[:dv] / a2[dv:dv + 1])
        ms = jnp.mean(o * o, axis=0, keepdims=True)
        y = o * lax.rsqrt(ms + EPS) * gain
        o_ref[0, pl.ds(pl.multiple_of(qi * tq, tq), tq), :] = y.T.astype(o_ref.dtype)

    _flash_either(bounded_ref, nq, nk, 2, qk, pv, finish, scratch)


def _diff_attn_call(bounded, qd, kd, vdt, lp, tq, tk):
    b, s, _ = qd.shape
    kern = functools.partial(_diff_attn_kernel, tq=tq, tk=tk)
    small = [lp['lq1'], lp['lk1'], lp['lq2'], lp['lk2'], lp['lam0'], lp['g_sub']]
    return pl.pallas_call(
        kern,
        grid=(b, DIFF_HEADS),
        in_specs=[
            pl.BlockSpec(memory_space=pltpu.SMEM),
            pl.BlockSpec((1, s, 2 * LANES), lambda bi, h: (bi, 0, h)),
            pl.BlockSpec((1, s, LANES), lambda bi, h: (bi, 0, h)),
            pl.BlockSpec((1, vdt.shape[1], DIFF_DV, vdt.shape[3]), lambda bi, h: (bi, 0, h, 0)),
        ] + [pl.BlockSpec(a.shape, lambda bi, h: (0, 0)) for a in small],
        out_specs=pl.BlockSpec((1, s, DIFF_DV), lambda bi, h: (bi, 0, h)),
        out_shape=jax.ShapeDtypeStruct((b, s, DIFF_HEADS * DIFF_DV), MXU_DTYPE),
        scratch_shapes=_flash_scratch(tq, tk, DIFF_DV + 16),
        compiler_params=pltpu.CompilerParams(
            dimension_semantics=("arbitrary", "arbitrary"), vmem_limit_bytes=VMEM_LIMIT),
        name="diff_attn",
    )(bounded, qd, kd, vdt, *small)


def _ffn_kernel(x_ref, xp_ref, xn_ref, am_ref, amp_ref, amn_ref, ad_ref, adp_ref, adn_ref,
                wom_ref, wod_ref, ln2_ref, wg_ref, wu_ref, cw_ref, wd_ref,
                o_ref, h_scr, g_scr):
    tm = x_ref.shape[1]
    halo = xp_ref.shape[1]
    ti = pl.program_id(1)
    nt = pl.num_programs(1)
    nchunk = wg_ref.shape[0]

    x_ext = jnp.concatenate([xp_ref[0], x_ref[0], xn_ref[0]], axis=0)
    am_ext = jnp.concatenate([amp_ref[0], am_ref[0], amn_ref[0]], axis=0)
    ad_ext = jnp.concatenate([adp_ref[0], ad_ref[0], adn_ref[0]], axis=0)
    xmid = x_ext + _dot(am_ext, wom_ref[...]) + _dot(ad_ext, wod_ref[...])
    h_scr[...] = _rms(xmid, ln2_ref[...]).astype(h_scr.dtype)
    o_ref[0] = xmid[halo:halo + tm]

    row = lax.broadcasted_iota(jnp.int32, (tm + 2 * halo, 1), 0)
    valid = jnp.logical_and(jnp.logical_or(row >= halo, ti > 0),
                            jnp.logical_or(row < halo + tm, ti < nt - 1))

    def chunk(c, carry):
        h_ext = h_scr[...]
        g = _dot(h_ext, wg_ref[c])
        g_scr[...] = jnp.where(valid, g, 0.0)
        u = _dot(h_ext[halo:halo + tm], wu_ref[c])
        cw = cw_ref[c]
        conv = (g_scr[pl.ds(halo - 1, tm), :] * cw[0:1] + g_scr[pl.ds(halo, tm), :] * cw[1:2]
                + g_scr[pl.ds(halo + 1, tm), :] * cw[2:3] + cw[3:4])
        a = (conv * jax.nn.sigmoid(conv) * u).astype(h_scr.dtype)
        o_ref[0] += _dot(a, wd_ref[c])
        return carry

    lax.fori_loop(0, nchunk, chunk, 0)


def _ffn_call(x, am, ad, lp, tm):
    b, s, d = x.shape
    nt = s // tm
    halo = FFN_HALO
    r = tm // halo
    nhb = s // halo

    def main(w):
        return pl.BlockSpec((1, tm, w), lambda bi, ti: (bi, ti, 0))

    def prev(w):
        return pl.BlockSpec((1, halo, w), lambda bi, ti: (bi, jnp.maximum(ti * r - 1, 0), 0))

    def nxt(w):
        return pl.BlockSpec((1, halo, w), lambda bi, ti: (bi, jnp.minimum((ti + 1) * r, nhb - 1), 0))

    def resident(a):
        return pl.BlockSpec(a.shape, lambda bi, ti: (0,) * a.ndim, pipeline_mode=pl.Buffered(1))

    weights = [lp['wo_m'], lp['wo_d'], lp['ln2'], lp['w_gate'], lp['w_up'], lp['conv'], lp['w_down']]
    wa = am.shape[-1]
    in_specs = ([main(d), prev(d), nxt(d), main(wa), prev(wa), nxt(wa), main(wa), prev(wa), nxt(wa)]
                + [resident(a) for a in weights])
    return pl.pallas_call(
        _ffn_kernel,
        grid=(b, nt),
        in_specs=in_specs,
        out_specs=main(d),
        out_shape=jax.ShapeDtypeStruct((b, s, d), jnp.float32),
        scratch_shapes=[pltpu.VMEM((tm + 2 * halo, d), MXU_DTYPE),
                        pltpu.VMEM((tm + 2 * halo, FFN_CHUNK), jnp.float32)],
        compiler_params=pltpu.CompilerParams(
            dimension_semantics=("arbitrary", "arbitrary"), vmem_limit_bytes=VMEM_LIMIT),
        name="ffn",
    )(x, x, x, am, am, am, ad, ad, ad, *weights)


def _rope_tables(s, half, theta, lane_starts):
    d = 2 * half
    freqs = theta ** (-jnp.arange(half, dtype=jnp.float32) * 2.0 / d)
    ang = jnp.arange(s, dtype=jnp.float32)[:, None] * freqs[None, :]
    cos, sin = jnp.cos(ang), jnp.sin(ang)
    cos_t = jnp.ones((s, LANES), jnp.float32)
    sa = jnp.zeros((s, LANES), jnp.float32)
    sb = jnp.zeros((s, LANES), jnp.float32)
    for st in lane_starts:
        cos_t = cos_t.at[:, st:st + half].set(cos).at[:, st + half:st + d].set(cos)
        sa = sa.at[:, st:st + half].set(-sin)
        sb = sb.at[:, st + half:st + d].set(sin)
    return cos_t, sa, sb


def _block_ones(width):
    idx = np.arange(256) // width
    return jnp.asarray(idx[:, None] == idx[None, :], MXU_DTYPE)


def _tables(s):
    cosm, sma, smb = _rope_tables(s, MLA_ROPE // 2, MLA_THETA, (MLA_NOPE,))
    cosd, sda, sdb = _rope_tables(s, DIFF_ROPE // 2, ROPE_THETA, (0, DIFF_DK))
    return dict(cosm=cosm, sma=sma, smb=smb, cosd=cosd, sda=sda, sdb=sdb,
                bo128=_block_ones(LANES), bo64=_block_ones(DIFF_DK))


def _layer_params(l, p):
    f32 = jnp.float32
    cd = MXU_DTYPE
    w_in = p['w_in'][l]
    o_cq, o_ckv, o_kpe = 0, MLA_Q_LORA, MLA_Q_LORA + MLA_KV_LORA
    o_dq = o_kpe + MLA_ROPE
    o_dk = o_dq + DIFF_HEADS * 2 * DIFF_DK
    o_dv = o_dk + DIFF_HEADS * 2 * DIFF_DK
    kpe_pad = jnp.zeros((D_MODEL, LANES), f32).at[:, MLA_NOPE:MLA_QK].set(w_in[:, o_kpe:o_dq])
    w_in_main = jnp.concatenate([w_in[:, o_cq:o_kpe], kpe_pad, w_in[:, o_dq:o_dv]], axis=1)
    w_q = p['w_q_up'][l].reshape(MLA_Q_LORA, MLA_HEADS, MLA_QK)
    w_q = jnp.pad(w_q, ((0, 0), (0, 0), (0, LANES - MLA_QK))).reshape(MLA_Q_LORA, MLA_HEADS * LANES)
    w_kv = p['w_kv_up'][l].reshape(MLA_KV_LORA, MLA_HEADS, MLA_NOPE + MLA_V)
    w_k = jnp.pad(w_kv[:, :, :MLA_NOPE], ((0, 0), (0, 0), (0, LANES - MLA_NOPE)))
    w_k = w_k.reshape(MLA_KV_LORA, MLA_HEADS * LANES)
    w_vt = w_kv[:, :, MLA_NOPE:].reshape(MLA_KV_LORA, MLA_HEADS * MLA_V).T
    nchunk = D_FF // FFN_CHUNK
    conv = jnp.concatenate([p['conv_w'][l], p['conv_b'][l][None, :],
                            jnp.zeros((4, D_FF), f32)], axis=0)
    lam0 = 0.8 - 0.6 * math.exp(-0.3 * l)
    row = lambda v: v.astype(f32)[None, :]

    def score_bounded(gq, gk, width):
        bound = (width ** 0.5 * LOG2E * NORM_SLACK) * jnp.max(jnp.abs(gq)) * jnp.max(jnp.abs(gk))
        return (bound <= SCORE_BOUND_LOG2).astype(jnp.int32).reshape(1)

    w_out = p['w_out'][l]
    return dict(
        mla_bounded=score_bounded(p['mla_qn_g'][l], p['mla_kn_g'][l], MLA_QK),
        diff_bounded=score_bounded(p['diff_qn_g'][l], p['diff_kn_g'][l], DIFF_DK),
        ln1=row(p['ln1_g'][l]),
        w_in=w_in_main.astype(cd),
        w_dvt=w_in[:, o_dv:].T.astype(cd),
        g_cq=row(p['mla_q_norm_g'][l]),
        w_q=w_q.astype(cd),
        g_ckv=row(p['mla_kv_norm_g'][l]),
        w_k=w_k.astype(cd),
        w_vt=w_vt.astype(cd),
        g_qn=row(jnp.pad(p['mla_qn_g'][l], (0, LANES - MLA_QK))),
        g_kn=row(jnp.pad(p['mla_kn_g'][l], (0, LANES - MLA_QK))),
        g_dq=row(jnp.tile(p['diff_qn_g'][l], 2)),
        g_dk=row(jnp.tile(p['diff_kn_g'][l], 2)),
        lq1=row(p['lambda_q1'][l]), lk1=row(p['lambda_k1'][l]),
        lq2=row(p['lambda_q2'][l]), lk2=row(p['lambda_k2'][l]),
        lam0=jnp.full((1, 1), lam0, f32),
        g_sub=p['diff_subln_g'][l].astype(f32)[:, None],
        wo_m=w_out[:MLA_HEADS * MLA_V].astype(cd),
        wo_d=w_out[MLA_HEADS * MLA_V:].astype(cd),
        ln2=row(p['ln2_g'][l]),
        w_gate=p['w_gate'][l].reshape(D_MODEL, nchunk, FFN_CHUNK).transpose(1, 0, 2).astype(cd),
        w_up=p['w_up'][l].reshape(D_MODEL, nchunk, FFN_CHUNK).transpose(1, 0, 2).astype(cd),
        conv=conv.reshape(8, nchunk, FFN_CHUNK).transpose(1, 0, 2),
        w_down=p['w_down'][l].reshape(nchunk, FFN_CHUNK, D_MODEL).astype(cd),
    )


def _trunk(x, params):
    b, s, _ = x.shape
    tq = min(ATT_TQ, s)
    tk = min(ATT_TK, s)
    tm_pre = min(PRE_TM, s)
    tm_ffn = min(FFN_TM, s)
    tabs = _tables(s)
    for l in range(DEPTH):
        lp = _layer_params(l, params)
        qm, km, vmt, qd, kd, vdt = _pre_call(x, lp, tabs, tm_pre, min(VT_TILE, s))
        am = _mla_attn_call(lp['mla_bounded'], qm, km, vmt, tq, tk)
        ad = _diff_attn_call(lp['diff_bounded'], qd, kd, vdt, lp, tq, tk)
        x = _ffn_call(x, am, ad, lp, tm_ffn)
    return x


def kernel(x_prompt, x_sample, ln1_g, w_in, mla_q_norm_g, w_q_up, mla_kv_norm_g, w_kv_up,
           mla_qn_g, mla_kn_g, diff_qn_g, diff_kn_g, lambda_q1, lambda_k1, lambda_q2,
           lambda_k2, diff_subln_g, w_out, ln2_g, w_gate, conv_w, conv_b, w_up, w_down):
    params = dict(ln1_g=ln1_g, w_in=w_in, mla_q_norm_g=mla_q_norm_g, w_q_up=w_q_up,
                  mla_kv_norm_g=mla_kv_norm_g, w_kv_up=w_kv_up, mla_qn_g=mla_qn_g,
                  mla_kn_g=mla_kn_g, diff_qn_g=diff_qn_g, diff_kn_g=diff_kn_g,
                  lambda_q1=lambda_q1, lambda_k1=lambda_k1, lambda_q2=lambda_q2,
                  lambda_k2=lambda_k2, diff_subln_g=diff_subln_g, w_out=w_out, ln2_g=ln2_g,
                  w_gate=w_gate, conv_w=conv_w, conv_b=conv_b, w_up=w_up, w_down=w_down)
    nb = x_prompt.shape[0]
    assert x_prompt.shape[1:] == x_sample.shape[1:]
    x = jnp.concatenate([x_prompt, x_sample], axis=0)
    y = _trunk(x, params)
    return (y[:nb], y[nb:])
```

```python
import functools
import math

import jax
import jax.numpy as jnp
import numpy as np
from jax import lax
from jax.experimental import pallas as pl
from jax.experimental.pallas import tpu as pltpu

D_MODEL = 1024
DEPTH = 4
MLA_HEADS = 8
MLA_Q_LORA = 256
MLA_KV_LORA = 128
MLA_NOPE = 64
MLA_ROPE = 32
MLA_V = 64
MLA_QK = MLA_NOPE + MLA_ROPE
MLA_THETA = 10000.0
DIFF_HEADS = 4
DIFF_DK = 64
DIFF_DV = 2 * DIFF_DK
DIFF_ROPE = DIFF_DK // 4
ROPE_THETA = 500000.0
D_FF = 2816
EPS = 1e-6

LANES = 128
MXU_DTYPE = jnp.bfloat16
VMEM_LIMIT = 56 * 1024 * 1024

PRE_TM = 512
ATT_TQ = 1024
ATT_TK = 1024
VT_TILE = 512
ATT_SUB = 256
FFN_TM = 1024
FFN_HALO = 16
FFN_CHUNK = 256
NEG_BIG = -1e30
LOG2E = math.log2(math.e)
SCORE_BOUND_LOG2 = 60.0
NORM_SLACK = 1.05

_NT = (((1,), (1,)), ((), ()))


def _dot(a, b):
    return jnp.dot(a, b, preferred_element_type=jnp.float32)


def _dot_nt(a, b):
    return lax.dot_general(a, b, _NT, preferred_element_type=jnp.float32)


def _rms(x, g):
    ms = jnp.mean(x * x, axis=-1, keepdims=True)
    return x * lax.rsqrt(ms + EPS) * g


def _group_rsqrt(xc, block_ones, width):
    ssq = _dot((xc * xc).astype(MXU_DTYPE), block_ones)
    return lax.rsqrt(ssq * (1.0 / width) + EPS)


def _rope_group(xg, cos, sin_a, sin_b, shift):
    return (xg * cos + pltpu.roll(xg, LANES - shift, 1) * sin_a
            + pltpu.roll(xg, shift, 1) * sin_b)


def _pre_kernel(x_ref, ln1_ref, win_ref, wdvt_ref, gcq_ref, wq_ref, gckv_ref, wk_ref, wvt_ref,
                gqn_ref, gkn_ref, gdq_ref, gdk_ref,
                cosm_ref, sma_ref, smb_ref, cosd_ref, sda_ref, sdb_ref,
                bo128_ref, bo64_ref,
                qm_ref, km_ref, vmt_ref, qd_ref, kd_ref, vdt_ref):
    tm = x_ref.shape[1]
    tk = vmt_ref.shape[3]
    x = x_ref[0]
    hb = _rms(x, ln1_ref[...]).astype(MXU_DTYPE)
    proj = _dot(hb, win_ref[...])
    c_q = proj[:, 0:256]
    c_kv = proj[:, 256:384]
    kpe = proj[:, 384:512]
    dq = proj[:, 512:1024]
    dk = proj[:, 1024:1536]

    dvt = _dot_nt(wdvt_ref[...], hb).astype(MXU_DTYPE)
    cqn = _rms(c_q, gcq_ref[...]).astype(MXU_DTYPE)
    ckvn = _rms(c_kv, gckv_ref[...]).astype(MXU_DTYPE)
    q = _dot(cqn, wq_ref[...])
    kn = _dot(ckvn, wk_ref[...])
    vt = _dot_nt(wvt_ref[...], ckvn).astype(MXU_DTYPE)
    for t in range(tm // tk):
        vmt_ref[0, t] = vt[:, t * tk:(t + 1) * tk]
        vdt_ref[0, t] = dvt[:, t * tk:(t + 1) * tk]

    bo128 = bo128_ref[...]
    bo64 = bo64_ref[...]
    cosm, sma, smb = cosm_ref[...], sma_ref[...], smb_ref[...]
    cosd, sda, sdb = cosd_ref[...], sda_ref[...], sdb_ref[...]
    gqn, gkn, gdq, gdk = gqn_ref[...], gkn_ref[...], gdq_ref[...], gdk_ref[...]
    kpe2 = jnp.concatenate([kpe, kpe], axis=1)
    q_scale = MLA_QK ** -0.5 * LOG2E
    d_scale = DIFF_DK ** -0.5 * LOG2E
    half_m = MLA_ROPE // 2
    half_d = DIFF_ROPE // 2
    lane = lax.broadcasted_iota(jnp.int32, (tm, LANES), 1)

    for c in range(MLA_HEADS // 2):
        sl = slice(c * 256, (c + 1) * 256)
        qc = q[:, sl]
        qc = qc * _group_rsqrt(qc, bo128, MLA_QK)
        kc = kn[:, sl] + kpe2
        kc = kc * _group_rsqrt(kc, bo128, MLA_QK)
        for g in range(2):
            gs = slice(g * LANES, (g + 1) * LANES)
            out = slice((2 * c + g) * LANES, (2 * c + g + 1) * LANES)
            qg = _rope_group(qc[:, gs] * gqn, cosm, sma, smb, half_m)
            qm_ref[0, :, out] = (qg * q_scale).astype(MXU_DTYPE)
            kg = _rope_group(kc[:, gs] * gkn, cosm, sma, smb, half_m)
            km_ref[0, :, out] = kg.astype(MXU_DTYPE)

    for c in range(DIFF_HEADS // 2):
        sl = slice(c * 256, (c + 1) * 256)
        qc = dq[:, sl]
        qc = qc * _group_rsqrt(qc, bo64, DIFF_DK)
        kc = dk[:, sl]
        kc = kc * _group_rsqrt(kc, bo64, DIFF_DK)
        for g in range(2):
            h = 2 * c + g
            gs = slice(g * LANES, (g + 1) * LANES)
            qg = _rope_group(qc[:, gs] * gdq, cosd, sda, sdb, half_d) * d_scale
            zero = jnp.zeros_like(qg)
            qd_ref[0, :, (2 * h) * LANES:(2 * h + 1) * LANES] = (
                jnp.where(lane < DIFF_DK, qg, zero).astype(MXU_DTYPE))
            qd_ref[0, :, (2 * h + 1) * LANES:(2 * h + 2) * LANES] = (
                jnp.where(lane >= DIFF_DK, qg, zero).astype(MXU_DTYPE))
            kg = _rope_group(kc[:, gs] * gdk, cosd, sda, sdb, half_d)
            kd_ref[0, :, h * LANES:(h + 1) * LANES] = kg.astype(MXU_DTYPE)


def _pre_call(x, lp, tabs, tm, tk):
    b, s, d = x.shape
    nt = s // tm
    tpb = tm // tk

    def full(a):
        return pl.BlockSpec(a.shape, lambda bi, ti: (0,) * a.ndim)

    def tab_spec():
        return pl.BlockSpec((tm, LANES), lambda bi, ti: (ti, 0))

    weights = [lp['ln1'], lp['w_in'], lp['w_dvt'], lp['g_cq'], lp['w_q'], lp['g_ckv'], lp['w_k'],
               lp['w_vt'], lp['g_qn'], lp['g_kn'], lp['g_dq'], lp['g_dk']]
    consts = [tabs['bo128'], tabs['bo64']]
    rope = [tabs['cosm'], tabs['sma'], tabs['smb'], tabs['cosd'], tabs['sda'], tabs['sdb']]
    in_specs = ([pl.BlockSpec((1, tm, d), lambda bi, ti: (bi, ti, 0))]
                + [full(a) for a in weights] + [tab_spec() for _ in rope] + [full(a) for a in consts])
    tok = lambda w: pl.BlockSpec((1, tm, w), lambda bi, ti: (bi, ti, 0))
    vts = pl.BlockSpec((1, tpb, 512, tk), lambda bi, ti: (bi, ti, 0, 0))
    out_shape = [
        jax.ShapeDtypeStruct((b, s, 1024), MXU_DTYPE),
        jax.ShapeDtypeStruct((b, s, 1024), MXU_DTYPE),
        jax.ShapeDtypeStruct((b, s // tk, 512, tk), MXU_DTYPE),
        jax.ShapeDtypeStruct((b, s, 1024), MXU_DTYPE),
        jax.ShapeDtypeStruct((b, s, 512), MXU_DTYPE),
        jax.ShapeDtypeStruct((b, s // tk, 512, tk), MXU_DTYPE),
    ]
    out_specs = [tok(1024), tok(1024), vts, tok(1024), tok(512), vts]
    return pl.pallas_call(
        _pre_kernel,
        grid=(b, nt),
        in_specs=in_specs,
        out_specs=out_specs,
        out_shape=out_shape,
        compiler_params=pltpu.CompilerParams(
            dimension_semantics=("arbitrary", "arbitrary"), vmem_limit_bytes=VMEM_LIMIT),
        name="proj",
    )(x, *weights, *rope, *consts)


def _flash_pipelined(nk, n_streams, qk_fn, pv_fn, s_scr, p_scr, tq, dvp):
    for st in range(n_streams):
        s_scr[st] = qk_fn(0, st)
        p_scr[st] = jnp.zeros(p_scr.shape[1:], p_scr.dtype)

    def body(j, carry):
        jn = jnp.minimum(j + 1, nk - 1)
        jp = jnp.maximum(j - 1, 0)
        out = []
        for st in range(n_streams):
            m, alpha, acc = carry[st]
            s_cur = s_scr[st]
            acc = acc * alpha + pv_fn(jp, st, p_scr[st])
            s_nxt = qk_fn(jn, st)
            m_new = jnp.maximum(m, jnp.max(s_cur, axis=0, keepdims=True))
            p_scr[st] = jnp.exp2(s_cur - m_new).astype(p_scr.dtype)
            s_scr[st] = s_nxt
            out.append((m_new, jnp.exp2(m - m_new), acc))
        return tuple(out)

    init = tuple((jnp.full((1, tq), NEG_BIG, jnp.float32), jnp.ones((1, tq), jnp.float32),
                  jnp.zeros((dvp, tq), jnp.float32)) for _ in range(n_streams))
    res = lax.fori_loop(0, nk, body, init)
    return [res[st][2] * res[st][1] + pv_fn(nk - 1, st, p_scr[st]) for st in range(n_streams)]


def _flash_bounded(nq, nk, n_streams, qk_fn, pv_fn, finish_fn, s_scr, p_scr, acc_scr):
    total = nq * nk
    tk, tq = p_scr.shape[1:]
    dv = acc_scr.shape[1] - 16
    for st in range(n_streams):
        p_scr[st] = jnp.zeros(p_scr.shape[1:], p_scr.dtype)
        acc_scr[st] = jnp.zeros(acc_scr.shape[1:], acc_scr.dtype)

    def body(t, c):
        ta = jnp.minimum(t, total - 1)
        tc = jnp.maximum(t - 1, 0)
        jc = tc % nk
        psums = [[] for _ in range(n_streams)]
        for sub in range(tq // ATT_SUB):
            cs = slice(sub * ATT_SUB, (sub + 1) * ATT_SUB)
            for st in range(n_streams):
                acc_scr[st, :dv, cs] += pv_fn(jc, st, p_scr[st, :, cs], False)
                p = jnp.exp2(qk_fn(ta // nk, ta % nk, st, sub))
                p_scr[st, :, cs] = p.astype(p_scr.dtype)
                psums[st].append(jnp.sum(p.reshape(tk // 8, 8, ATT_SUB), axis=0))

        @pl.when(jnp.logical_and(t >= 1, jc == nk - 1))
        def _():
            accs = [acc_scr[st] for st in range(n_streams)]
            finish_fn(tc // nk, [(a[:dv], jnp.sum(a[dv:dv + 8], axis=0, keepdims=True)) for a in accs])
            for st in range(n_streams):
                acc_scr[st] = jnp.zeros(acc_scr.shape[1:], acc_scr.dtype)

        for st in range(n_streams):
            acc_scr[st, dv:dv + 8, :] += jnp.concatenate(psums[st], axis=1)
        return c

    lax.fori_loop(0, total + 1, body, 0)


def _q_rows(qi, sub, tq):
    if sub is None:
        return pl.ds(pl.multiple_of(qi * tq, tq), tq)
    return pl.ds(pl.multiple_of(qi * tq + sub * ATT_SUB, ATT_SUB), ATT_SUB)


def _vt_tile(vt_ref, j, tk, rows):
    tv = vt_ref.shape[3]
    n = tk // tv
    parts = [vt_ref[0, j * n + i, rows, :] for i in range(n)]
    return parts[0] if n == 1 else jnp.concatenate(parts, axis=1)


def _flash_scratch(tq, tk, dvp, n_streams=2):
    return [pltpu.VMEM((n_streams, tk, tq), jnp.float32), pltpu.VMEM((n_streams, tk, tq), MXU_DTYPE),
            pltpu.VMEM((n_streams, dvp, tq), jnp.float32)]


def _flash_either(bounded_ref, nq, nk, n_streams, qk_fn, pv_fn, finish_fn, scratch):
    s_scr, p_scr, acc_scr = scratch
    tq = s_scr.shape[2]
    dvp = acc_scr.shape[1]
    dv = dvp - 16

    @pl.when(bounded_ref[0] != 0)
    def _():
        _flash_bounded(nq, nk, n_streams, qk_fn, pv_fn, finish_fn, s_scr, p_scr, acc_scr)

    @pl.when(bounded_ref[0] == 0)
    def _():
        def qbody(qi, c):
            accs = _flash_pipelined(nk, n_streams, lambda j, st: qk_fn(qi, j, st, None),
                                    lambda j, st, p: pv_fn(j, st, p, True), s_scr, p_scr, tq, dvp)
            finish_fn(qi, [(a[:dv], a[dv:dv + 1]) for a in accs])
            return c

        lax.fori_loop(0, nq, qbody, 0)


def _mla_attn_kernel(bounded_ref, q_ref, k_ref, vt_ref, o_ref, *scratch, tq, tk):
    s_len = q_ref.shape[1]
    nq, nk = s_len // tq, s_len // tk
    ones = jnp.ones((16, tk), MXU_DTYPE)
    dv = MLA_V

    def qk(qi, j, h, sub):
        k = k_ref[0, pl.ds(pl.multiple_of(j * tk, tk), tk), h * LANES:(h + 1) * LANES]
        return _dot_nt(k, q_ref[0, _q_rows(qi, sub, tq), h * LANES:(h + 1) * LANES])

    def pv(j, h, p, with_ones):
        vt = _vt_tile(vt_ref, j, tk, slice(h * dv, (h + 1) * dv))
        return _dot(jnp.concatenate([vt, ones], axis=0) if with_ones else vt, p)

    def finish(qi, acc_l):
        o = jnp.concatenate([a / l for a, l in acc_l], axis=0)
        o_ref[0, pl.ds(pl.multiple_of(qi * tq, tq), tq), :] = o.T.astype(o_ref.dtype)

    _flash_either(bounded_ref, nq, nk, 2, qk, pv, finish, scratch)


def _mla_attn_call(bounded, qm, km, vmt, tq, tk):
    b, s, _ = qm.shape
    kern = functools.partial(_mla_attn_kernel, tq=tq, tk=tk)
    return pl.pallas_call(
        kern,
        grid=(b, MLA_HEADS // 2),
        in_specs=[
            pl.BlockSpec(memory_space=pltpu.SMEM),
            pl.BlockSpec((1, s, 2 * LANES), lambda bi, hp: (bi, 0, hp)),
            pl.BlockSpec((1, s, 2 * LANES), lambda bi, hp: (bi, 0, hp)),
            pl.BlockSpec((1, vmt.shape[1], 2 * MLA_V, vmt.shape[3]), lambda bi, hp: (bi, 0, hp, 0)),
        ],
        out_specs=pl.BlockSpec((1, s, 2 * MLA_V), lambda bi, hp: (bi, 0, hp)),
        out_shape=jax.ShapeDtypeStruct((b, s, MLA_HEADS * MLA_V), MXU_DTYPE),
        scratch_shapes=_flash_scratch(tq, tk, MLA_V + 16),
        compiler_params=pltpu.CompilerParams(
            dimension_semantics=("arbitrary", "arbitrary"), vmem_limit_bytes=VMEM_LIMIT),
        name="mla_attn",
    )(bounded, qm, km, vmt)


def _diff_attn_kernel(bounded_ref, q_ref, k_ref, vt_ref, lq1_ref, lk1_ref, lq2_ref, lk2_ref,
                      lam0_ref, g_ref, o_ref, *scratch, tq, tk):
    s_len = q_ref.shape[1]
    nq, nk = s_len // tq, s_len // tk
    ones = jnp.ones((16, tk), MXU_DTYPE)
    dv = DIFF_DV
    lam0 = lam0_ref[...]
    lam = (jnp.exp(jnp.sum(lq1_ref[...] * lk1_ref[...], axis=-1, keepdims=True))
           - jnp.exp(jnp.sum(lq2_ref[...] * lk2_ref[...], axis=-1, keepdims=True)) + lam0)
    gain = g_ref[...] * (1.0 - lam0)

    def qk(qi, j, mi, sub):
        k = k_ref[0, pl.ds(pl.multiple_of(j * tk, tk), tk), :]
        return _dot_nt(k, q_ref[0, _q_rows(qi, sub, tq), mi * LANES:(mi + 1) * LANES])

    def pv(j, mi, p, with_ones):
        vt = _vt_tile(vt_ref, j, tk, slice(None))
        return _dot(jnp.concatenate([vt, ones], axis=0) if with_ones else vt, p)

    def finish(qi, acc_l):
        (a1, l1), (a2, l2) = acc_l
        o = a1 / l1 - lam * (a2 / l2)
        ms = jnp.mean(o * o, axis=0, keepdims=True)
        y = o * lax.rsqrt(ms + EPS) * gain
        o_ref[0, pl.ds(pl.multiple_of(qi * tq, tq), tq), :] = y.T.astype(o_ref.dtype)

    _flash_either(bounded_ref, nq, nk, 2, qk, pv, finish, scratch)


def _diff_attn_call(bounded, qd, kd, vdt, lp, tq, tk):
    b, s, _ = qd.shape
    kern = functools.partial(_diff_attn_kernel, tq=tq, tk=tk)
    small = [lp['lq1'], lp['lk1'], lp['lq2'], lp['lk2'], lp['lam0'], lp['g_sub']]
    return pl.pallas_call(
        kern,
        grid=(b, DIFF_HEADS),
        in_specs=[
            pl.BlockSpec(memory_space=pltpu.SMEM),
            pl.BlockSpec((1, s, 2 * LANES), lambda bi, h: (bi, 0, h)),
            pl.BlockSpec((1, s, LANES), lambda bi, h: (bi, 0, h)),
            pl.BlockSpec((1, vdt.shape[1], DIFF_DV, vdt.shape[3]), lambda bi, h: (bi, 0, h, 0)),
        ] + [pl.BlockSpec(a.shape, lambda bi, h: (0, 0)) for a in small],
        out_specs=pl.BlockSpec((1, s, DIFF_DV), lambda bi, h: (bi, 0, h)),
        out_shape=jax.ShapeDtypeStruct((b, s, DIFF_HEADS * DIFF_DV), MXU_DTYPE),
        scratch_shapes=_flash_scratch(tq, tk, DIFF_DV + 16),
        compiler_params=pltpu.CompilerParams(
            dimension_semantics=("arbitrary", "arbitrary"), vmem_limit_bytes=VMEM_LIMIT),
        name="diff_attn",
    )(bounded, qd, kd, vdt, *small)


def _ffn_kernel(x_ref, xp_ref, xn_ref, am_ref, amp_ref, amn_ref, ad_ref, adp_ref, adn_ref,
                wom_ref, wod_ref, ln2_ref, wg_ref, wu_ref, cw_ref, wd_ref,
                o_ref, h_scr, g_scr):
    tm = x_ref.shape[1]
    halo = xp_ref.shape[1]
    ti = pl.program_id(1)
    nt = pl.num_programs(1)
    nchunk = wg_ref.shape[0]

    x_ext = jnp.concatenate([xp_ref[0], x_ref[0], xn_ref[0]], axis=0)
    am_ext = jnp.concatenate([amp_ref[0], am_ref[0], amn_ref[0]], axis=0)
    ad_ext = jnp.concatenate([adp_ref[0], ad_ref[0], adn_ref[0]], axis=0)
    xmid = x_ext + _dot(am_ext, wom_ref[...]) + _dot(ad_ext, wod_ref[...])
    h_scr[...] = _rms(xmid, ln2_ref[...]).astype(h_scr.dtype)
    o_ref[0] = xmid[halo:halo + tm]

    row = lax.broadcasted_iota(jnp.int32, (tm + 2 * halo, 1), 0)
    valid = jnp.logical_and(jnp.logical_or(row >= halo, ti > 0),
                            jnp.logical_or(row < halo + tm, ti < nt - 1))

    def chunk(c, carry):
        h_ext = h_scr[...]
        g = _dot(h_ext, wg_ref[c])
        g_scr[...] = jnp.where(valid, g, 0.0)
        u = _dot(h_ext[halo:halo + tm], wu_ref[c])
        cw = cw_ref[c]
        conv = (g_scr[pl.ds(halo - 1, tm), :] * cw[0:1] + g_scr[pl.ds(halo, tm), :] * cw[1:2]
                + g_scr[pl.ds(halo + 1, tm), :] * cw[2:3] + cw[3:4])
        a = (conv * jax.nn.sigmoid(conv) * u).astype(h_scr.dtype)
        o_ref[0] += _dot(a, wd_ref[c])
        return carry

    lax.fori_loop(0, nchunk, chunk, 0)


def _ffn_call(x, am, ad, lp, tm):
    b, s, d = x.shape
    nt = s // tm
    halo = FFN_HALO
    r = tm // halo
    nhb = s // halo

    def main(w):
        return pl.BlockSpec((1, tm, w), lambda bi, ti: (bi, ti, 0))

    def prev(w):
        return pl.BlockSpec((1, halo, w), lambda bi, ti: (bi, jnp.maximum(ti * r - 1, 0), 0))

    def nxt(w):
        return pl.BlockSpec((1, halo, w), lambda bi, ti: (bi, jnp.minimum((ti + 1) * r, nhb - 1), 0))

    def resident(a):
        return pl.BlockSpec(a.shape, lambda bi, ti: (0,) * a.ndim, pipeline_mode=pl.Buffered(1))

    weights = [lp['wo_m'], lp['wo_d'], lp['ln2'], lp['w_gate'], lp['w_up'], lp['conv'], lp['w_down']]
    wa = am.shape[-1]
    in_specs = ([main(d), prev(d), nxt(d), main(wa), prev(wa), nxt(wa), main(wa), prev(wa), nxt(wa)]
                + [resident(a) for a in weights])
    return pl.pallas_call(
        _ffn_kernel,
        grid=(b, nt),
        in_specs=in_specs,
        out_specs=main(d),
        out_shape=jax.ShapeDtypeStruct((b, s, d), jnp.float32),
        scratch_shapes=[pltpu.VMEM((tm + 2 * halo, d), MXU_DTYPE),
                        pltpu.VMEM((tm + 2 * halo, FFN_CHUNK), jnp.float32)],
        compiler_params=pltpu.CompilerParams(
            dimension_semantics=("arbitrary", "arbitrary"), vmem_limit_bytes=VMEM_LIMIT),
        name="ffn",
    )(x, x, x, am, am, am, ad, ad, ad, *weights)


def _rope_tables(s, half, theta, lane_starts):
    d = 2 * half
    freqs = theta ** (-jnp.arange(half, dtype=jnp.float32) * 2.0 / d)
    ang = jnp.arange(s, dtype=jnp.float32)[:, None] * freqs[None, :]
    cos, sin = jnp.cos(ang), jnp.sin(ang)
    cos_t = jnp.ones((s, LANES), jnp.float32)
    sa = jnp.zeros((s, LANES), jnp.float32)
    sb = jnp.zeros((s, LANES), jnp.float32)
    for st in lane_starts:
        cos_t = cos_t.at[:, st:st + half].set(cos).at[:, st + half:st + d].set(cos)
        sa = sa.at[:, st:st + half].set(-sin)
        sb = sb.at[:, st + half:st + d].set(sin)
    return cos_t, sa, sb


def _block_ones(width):
    idx = np.arange(256) // width
    return jnp.asarray(idx[:, None] == idx[None, :], MXU_DTYPE)


def _tables(s):
    cosm, sma, smb = _rope_tables(s, MLA_ROPE // 2, MLA_THETA, (MLA_NOPE,))
    cosd, sda, sdb = _rope_tables(s, DIFF_ROPE // 2, ROPE_THETA, (0, DIFF_DK))
    return dict(cosm=cosm, sma=sma, smb=smb, cosd=cosd, sda=sda, sdb=sdb,
                bo128=_block_ones(LANES), bo64=_block_ones(DIFF_DK))


def _layer_params(l, p):
    f32 = jnp.float32
    cd = MXU_DTYPE
    w_in = p['w_in'][l]
    o_cq, o_ckv, o_kpe = 0, MLA_Q_LORA, MLA_Q_LORA + MLA_KV_LORA
    o_dq = o_kpe + MLA_ROPE
    o_dk = o_dq + DIFF_HEADS * 2 * DIFF_DK
    o_dv = o_dk + DIFF_HEADS * 2 * DIFF_DK
    kpe_pad = jnp.zeros((D_MODEL, LANES), f32).at[:, MLA_NOPE:MLA_QK].set(w_in[:, o_kpe:o_dq])
    w_in_main = jnp.concatenate([w_in[:, o_cq:o_kpe], kpe_pad, w_in[:, o_dq:o_dv]], axis=1)
    w_q = p['w_q_up'][l].reshape(MLA_Q_LORA, MLA_HEADS, MLA_QK)
    w_q = jnp.pad(w_q, ((0, 0), (0, 0), (0, LANES - MLA_QK))).reshape(MLA_Q_LORA, MLA_HEADS * LANES)
    w_kv = p['w_kv_up'][l].reshape(MLA_KV_LORA, MLA_HEADS, MLA_NOPE + MLA_V)
    w_k = jnp.pad(w_kv[:, :, :MLA_NOPE], ((0, 0), (0, 0), (0, LANES - MLA_NOPE)))
    w_k = w_k.reshape(MLA_KV_LORA, MLA_HEADS * LANES)
    w_vt = w_kv[:, :, MLA_NOPE:].reshape(MLA_KV_LORA, MLA_HEADS * MLA_V).T
    nchunk = D_FF // FFN_CHUNK
    conv = jnp.concatenate([p['conv_w'][l], p['conv_b'][l][None, :],
                            jnp.zeros((4, D_FF), f32)], axis=0)
    lam0 = 0.8 - 0.6 * math.exp(-0.3 * l)
    row = lambda v: v.astype(f32)[None, :]

    def score_bounded(gq, gk, width):
        bound = (width ** 0.5 * LOG2E * NORM_SLACK) * jnp.max(jnp.abs(gq)) * jnp.max(jnp.abs(gk))
        return (bound <= SCORE_BOUND_LOG2).astype(jnp.int32).reshape(1)

    w_out = p['w_out'][l]
    return dict(
        mla_bounded=score_bounded(p['mla_qn_g'][l], p['mla_kn_g'][l], MLA_QK),
        diff_bounded=score_bounded(p['diff_qn_g'][l], p['diff_kn_g'][l], DIFF_DK),
        ln1=row(p['ln1_g'][l]),
        w_in=w_in_main.astype(cd),
        w_dvt=w_in[:, o_dv:].T.astype(cd),
        g_cq=row(p['mla_q_norm_g'][l]),
        w_q=w_q.astype(cd),
        g_ckv=row(p['mla_kv_norm_g'][l]),
        w_k=w_k.astype(cd),
        w_vt=w_vt.astype(cd),
        g_qn=row(jnp.pad(p['mla_qn_g'][l], (0, LANES - MLA_QK))),
        g_kn=row(jnp.pad(p['mla_kn_g'][l], (0, LANES - MLA_QK))),
        g_dq=row(jnp.tile(p['diff_qn_g'][l], 2)),
        g_dk=row(jnp.tile(p['diff_kn_g'][l], 2)),
        lq1=row(p['lambda_q1'][l]), lk1=row(p['lambda_k1'][l]),
        lq2=row(p['lambda_q2'][l]), lk2=row(p['lambda_k2'][l]),
        lam0=jnp.full((1, 1), lam0, f32),
        g_sub=p['diff_subln_g'][l].astype(f32)[:, None],
        wo_m=w_out[:MLA_HEADS * MLA_V].astype(cd),
        wo_d=w_out[MLA_HEADS * MLA_V:].astype(cd),
        ln2=row(p['ln2_g'][l]),
        w_gate=p['w_gate'][l].reshape(D_MODEL, nchunk, FFN_CHUNK).transpose(1, 0, 2).astype(cd),
        w_up=p['w_up'][l].reshape(D_MODEL, nchunk, FFN_CHUNK).transpose(1, 0, 2).astype(cd),
        conv=conv.reshape(8, nchunk, FFN_CHUNK).transpose(1, 0, 2),
        w_down=p['w_down'][l].reshape(nchunk, FFN_CHUNK, D_MODEL).astype(cd),
    )


def _trunk(x, layers):
    b, s, _ = x.shape
    tq = min(ATT_TQ, s)
    tk = min(ATT_TK, s)
    tm_pre = min(PRE_TM, s)
    tm_ffn = min(FFN_TM, s)
    tabs = _tables(s)
    for lp in layers:
        qm, km, vmt, qd, kd, vdt = _pre_call(x, lp, tabs, tm_pre, min(VT_TILE, s))
        am = _mla_attn_call(lp['mla_bounded'], qm, km, vmt, tq, tk)
        ad = _diff_attn_call(lp['diff_bounded'], qd, kd, vdt, lp, tq, tk)
        x = _ffn_call(x, am, ad, lp, tm_ffn)
    return x


def kernel(x_prompt, x_sample, ln1_g, w_in, mla_q_norm_g, w_q_up, mla_kv_norm_g, w_kv_up,
           mla_qn_g, mla_kn_g, diff_qn_g, diff_kn_g, lambda_q1, lambda_k1, lambda_q2,
           lambda_k2, diff_subln_g, w_out, ln2_g, w_gate, conv_w, conv_b, w_up, w_down):
    params = dict(ln1_g=ln1_g, w_in=w_in, mla_q_norm_g=mla_q_norm_g, w_q_up=w_q_up,
                  mla_kv_norm_g=mla_kv_norm_g, w_kv_up=w_kv_up, mla_qn_g=mla_qn_g,
                  mla_kn_g=mla_kn_g, diff_qn_g=diff_qn_g, diff_kn_g=diff_kn_g,
                  lambda_q1=lambda_q1, lambda_k1=lambda_k1, lambda_q2=lambda_q2,
                  lambda_k2=lambda_k2, diff_subln_g=diff_subln_g, w_out=w_out, ln2_g=ln2_g,
                  w_gate=w_gate, conv_w=conv_w, conv_b=conv_b, w_up=w_up, w_down=w_down)
    layers = [_layer_params(l, params) for l in range(DEPTH)]
    return (_trunk(x_prompt, layers), _trunk(x_sample, layers))
```

```python
import functools
import math

import jax
import jax.numpy as jnp
import numpy as np
from jax import lax
from jax.experimental import pallas as pl
from jax.experimental.pallas import tpu as pltpu

D_MODEL = 1024
DEPTH = 4
MLA_HEADS = 8
MLA_Q_LORA = 256
MLA_KV_LORA = 128
MLA_NOPE = 64
MLA_ROPE = 32
MLA_V = 64
MLA_QK = MLA_NOPE + MLA_ROPE
MLA_THETA = 10000.0
DIFF_HEADS = 4
DIFF_DK = 64
DIFF_DV = 2 * DIFF_DK
DIFF_ROPE = DIFF_DK // 4
ROPE_THETA = 500000.0
D_FF = 2816
EPS = 1e-6

LANES = 128
MXU_DTYPE = jnp.bfloat16
VMEM_LIMIT = 56 * 1024 * 1024

PRE_TM = 512
ATT_TQ = 2048
ATT_TQ_ONLINE = 1024
ATT_TK = 1024
VT_TILE = 512
ATT_SUB = 256
FFN_TM = 1024
FFN_HALO = 16
FFN_CHUNK = 256
NEG_BIG = -1e30
LOG2E = math.log2(math.e)
SCORE_BOUND_LOG2 = 60.0
NORM_SLACK = 1.05

_NT = (((1,), (1,)), ((), ()))


def _dot(a, b):
    return jnp.dot(a, b, preferred_element_type=jnp.float32)


def _dot_nt(a, b):
    return lax.dot_general(a, b, _NT, preferred_element_type=jnp.float32)


def _rms(x, g):
    ms = jnp.mean(x * x, axis=-1, keepdims=True)
    return x * lax.rsqrt(ms + EPS) * g


def _group_rsqrt(xc, block_ones, width):
    ssq = _dot((xc * xc).astype(MXU_DTYPE), block_ones)
    return lax.rsqrt(ssq * (1.0 / width) + EPS)


def _rope_group(xg, cos, sin_a, sin_b, shift):
    return (xg * cos + pltpu.roll(xg, LANES - shift, 1) * sin_a
            + pltpu.roll(xg, shift, 1) * sin_b)


def _pre_kernel(x_ref, ln1_ref, win_ref, wdvt_ref, gcq_ref, wq_ref, gckv_ref, wk_ref, wvt_ref,
                gqn_ref, gkn_ref, gdq_ref, gdk_ref,
                cosm_ref, sma_ref, smb_ref, cosd_ref, sda_ref, sdb_ref,
                bo128_ref, bo64_ref,
                qm_ref, km_ref, vmt_ref, qd_ref, kd_ref, vdt_ref):
    tm = x_ref.shape[1]
    tk = vmt_ref.shape[3]
    x = x_ref[0]
    hb = _rms(x, ln1_ref[...]).astype(MXU_DTYPE)
    proj = _dot(hb, win_ref[...])
    c_q = proj[:, 0:256]
    c_kv = proj[:, 256:384]
    kpe = proj[:, 384:512]
    dq = proj[:, 512:1024]
    dk = proj[:, 1024:1536]

    dvt = _dot_nt(wdvt_ref[...], hb).astype(MXU_DTYPE)
    cqn = _rms(c_q, gcq_ref[...]).astype(MXU_DTYPE)
    ckvn = _rms(c_kv, gckv_ref[...]).astype(MXU_DTYPE)
    q = _dot(cqn, wq_ref[...])
    kn = _dot(ckvn, wk_ref[...])
    vt = _dot_nt(wvt_ref[...], ckvn).astype(MXU_DTYPE)
    for t in range(tm // tk):
        vmt_ref[0, t] = vt[:, t * tk:(t + 1) * tk]
        vdt_ref[0, t] = dvt[:, t * tk:(t + 1) * tk]

    bo128 = bo128_ref[...]
    bo64 = bo64_ref[...]
    cosm, sma, smb = cosm_ref[...], sma_ref[...], smb_ref[...]
    cosd, sda, sdb = cosd_ref[...], sda_ref[...], sdb_ref[...]
    gqn, gkn, gdq, gdk = gqn_ref[...], gkn_ref[...], gdq_ref[...], gdk_ref[...]
    kpe2 = jnp.concatenate([kpe, kpe], axis=1)
    q_scale = MLA_QK ** -0.5 * LOG2E
    d_scale = DIFF_DK ** -0.5 * LOG2E
    half_m = MLA_ROPE // 2
    half_d = DIFF_ROPE // 2
    lane = lax.broadcasted_iota(jnp.int32, (tm, LANES), 1)

    for c in range(MLA_HEADS // 2):
        sl = slice(c * 256, (c + 1) * 256)
        qc = q[:, sl]
        qc = qc * _group_rsqrt(qc, bo128, MLA_QK)
        kc = kn[:, sl] + kpe2
        kc = kc * _group_rsqrt(kc, bo128, MLA_QK)
        for g in range(2):
            gs = slice(g * LANES, (g + 1) * LANES)
            out = slice((2 * c + g) * LANES, (2 * c + g + 1) * LANES)
            qg = _rope_group(qc[:, gs] * gqn, cosm, sma, smb, half_m)
            qm_ref[0, :, out] = (qg * q_scale).astype(MXU_DTYPE)
            kg = _rope_group(kc[:, gs] * gkn, cosm, sma, smb, half_m)
            km_ref[0, :, out] = kg.astype(MXU_DTYPE)

    for c in range(DIFF_HEADS // 2):
        sl = slice(c * 256, (c + 1) * 256)
        qc = dq[:, sl]
        qc = qc * _group_rsqrt(qc, bo64, DIFF_DK)
        kc = dk[:, sl]
        kc = kc * _group_rsqrt(kc, bo64, DIFF_DK)
        for g in range(2):
            h = 2 * c + g
            gs = slice(g * LANES, (g + 1) * LANES)
            qg = _rope_group(qc[:, gs] * gdq, cosd, sda, sdb, half_d) * d_scale
            zero = jnp.zeros_like(qg)
            qd_ref[0, :, (2 * h) * LANES:(2 * h + 1) * LANES] = (
                jnp.where(lane < DIFF_DK, qg, zero).astype(MXU_DTYPE))
            qd_ref[0, :, (2 * h + 1) * LANES:(2 * h + 2) * LANES] = (
                jnp.where(lane >= DIFF_DK, qg, zero).astype(MXU_DTYPE))
            kg = _rope_group(kc[:, gs] * gdk, cosd, sda, sdb, half_d)
            kd_ref[0, :, h * LANES:(h + 1) * LANES] = kg.astype(MXU_DTYPE)


def _pre_call(x, lp, tabs, tm, tk):
    b, s, d = x.shape
    nt = s // tm
    tpb = tm // tk

    def full(a):
        return pl.BlockSpec(a.shape, lambda bi, ti: (0,) * a.ndim)

    def tab_spec():
        return pl.BlockSpec((tm, LANES), lambda bi, ti: (ti, 0))

    weights = [lp['ln1'], lp['w_in'], lp['w_dvt'], lp['g_cq'], lp['w_q'], lp['g_ckv'], lp['w_k'],
               lp['w_vt'], lp['g_qn'], lp['g_kn'], lp['g_dq'], lp['g_dk']]
    consts = [tabs['bo128'], tabs['bo64']]
    rope = [tabs['cosm'], tabs['sma'], tabs['smb'], tabs['cosd'], tabs['sda'], tabs['sdb']]
    in_specs = ([pl.BlockSpec((1, tm, d), lambda bi, ti: (bi, ti, 0))]
                + [full(a) for a in weights] + [tab_spec() for _ in rope] + [full(a) for a in consts])
    tok = lambda w: pl.BlockSpec((1, tm, w), lambda bi, ti: (bi, ti, 0))
    vts = pl.BlockSpec((1, tpb, 512, tk), lambda bi, ti: (bi, ti, 0, 0))
    out_shape = [
        jax.ShapeDtypeStruct((b, s, 1024), MXU_DTYPE),
        jax.ShapeDtypeStruct((b, s, 1024), MXU_DTYPE),
        jax.ShapeDtypeStruct((b, s // tk, 512, tk), MXU_DTYPE),
        jax.ShapeDtypeStruct((b, s, 1024), MXU_DTYPE),
        jax.ShapeDtypeStruct((b, s, 512), MXU_DTYPE),
        jax.ShapeDtypeStruct((b, s // tk, 512, tk), MXU_DTYPE),
    ]
    out_specs = [tok(1024), tok(1024), vts, tok(1024), tok(512), vts]
    return pl.pallas_call(
        _pre_kernel,
        grid=(b, nt),
        in_specs=in_specs,
        out_specs=out_specs,
        out_shape=out_shape,
        compiler_params=pltpu.CompilerParams(
            dimension_semantics=("arbitrary", "arbitrary"), vmem_limit_bytes=VMEM_LIMIT),
        name="proj",
    )(x, *weights, *rope, *consts)


def _flash_pipelined(nk, n_streams, qk_fn, pv_fn, s_scr, p_scr, tq, dvp):
    for st in range(n_streams):
        s_scr[st] = qk_fn(0, st)
        p_scr[st] = jnp.zeros(p_scr.shape[1:], p_scr.dtype)

    def body(j, carry):
        jn = jnp.minimum(j + 1, nk - 1)
        jp = jnp.maximum(j - 1, 0)
        out = []
        for st in range(n_streams):
            m, alpha, acc = carry[st]
            s_cur = s_scr[st]
            acc = acc * alpha + pv_fn(jp, st, p_scr[st])
            s_nxt = qk_fn(jn, st)
            m_new = jnp.maximum(m, jnp.max(s_cur, axis=0, keepdims=True))
            p_scr[st] = jnp.exp2(s_cur - m_new).astype(p_scr.dtype)
            s_scr[st] = s_nxt
            out.append((m_new, jnp.exp2(m - m_new), acc))
        return tuple(out)

    init = tuple((jnp.full((1, tq), NEG_BIG, jnp.float32), jnp.ones((1, tq), jnp.float32),
                  jnp.zeros((dvp, tq), jnp.float32)) for _ in range(n_streams))
    res = lax.fori_loop(0, nk, body, init)
    return [res[st][2] * res[st][1] + pv_fn(nk - 1, st, p_scr[st]) for st in range(n_streams)]


def _flash_bounded(nq, nk, n_streams, qk_fn, pv_fn, finish_fn, p_scr, acc_scr):
    total = nq * nk
    tk, tq = p_scr.shape[1:]
    dv = acc_scr.shape[1] - 16
    for st in range(n_streams):
        p_scr[st] = jnp.zeros(p_scr.shape[1:], p_scr.dtype)
        acc_scr[st] = jnp.zeros(acc_scr.shape[1:], acc_scr.dtype)

    def body(t, c):
        ta = jnp.minimum(t, total - 1)
        tc = jnp.maximum(t - 1, 0)
        jc = tc % nk
        psums = [[] for _ in range(n_streams)]
        for sub in range(tq // ATT_SUB):
            cs = slice(sub * ATT_SUB, (sub + 1) * ATT_SUB)
            for st in range(n_streams):
                acc_scr[st, :dv, cs] += pv_fn(jc, st, p_scr[st, :, cs], False)
                p = jnp.exp2(qk_fn((ta // nk) * tq + sub * ATT_SUB, ATT_SUB, ta % nk, st))
                p_scr[st, :, cs] = p.astype(p_scr.dtype)
                psums[st].append(jnp.sum(p.reshape(tk // 8, 8, ATT_SUB), axis=0))

        @pl.when(jnp.logical_and(t >= 1, jc == nk - 1))
        def _():
            accs = [acc_scr[st] for st in range(n_streams)]
            finish_fn((tc // nk) * tq, tq,
                      [(a[:dv], jnp.sum(a[dv:dv + 8], axis=0, keepdims=True)) for a in accs])
            for st in range(n_streams):
                acc_scr[st] = jnp.zeros(acc_scr.shape[1:], acc_scr.dtype)

        for st in range(n_streams):
            acc_scr[st, dv:dv + 8, :] += jnp.concatenate(psums[st], axis=1)
        return c

    lax.fori_loop(0, total + 1, body, 0)


def _rows(row0, width):
    return pl.ds(pl.multiple_of(row0, width), width)


def _vt_tile(vt_ref, j, tk, rows):
    tv = vt_ref.shape[3]
    n = tk // tv
    parts = [vt_ref[0, j * n + i, rows, :] for i in range(n)]
    return parts[0] if n == 1 else jnp.concatenate(parts, axis=1)


def _flash_scratch(tq, tk, dvp, n_streams=2):
    return [pltpu.VMEM((n_streams, tk, min(tq, ATT_TQ_ONLINE)), jnp.float32),
            pltpu.VMEM((n_streams, tk, tq), MXU_DTYPE), pltpu.VMEM((n_streams, dvp, tq), jnp.float32)]


def _flash_either(bounded_ref, s_len, n_streams, qk_fn, pv_fn, finish_fn, scratch):
    s_scr, p_scr, acc_scr = scratch
    tk, tq = p_scr.shape[1:]
    tq_on = s_scr.shape[2]
    nk = s_len // tk
    dvp = acc_scr.shape[1]
    dv = dvp - 16

    @pl.when(bounded_ref[0] != 0)
    def _():
        _flash_bounded(s_len // tq, nk, n_streams, qk_fn, pv_fn, finish_fn, p_scr, acc_scr)

    @pl.when(bounded_ref[0] == 0)
    def _():
        def qbody(qi, c):
            accs = _flash_pipelined(nk, n_streams, lambda j, st: qk_fn(qi * tq_on, tq_on, j, st),
                                    lambda j, st, p: pv_fn(j, st, p, True),
                                    s_scr, p_scr.at[:, :, 0:tq_on], tq_on, dvp)
            finish_fn(qi * tq_on, tq_on, [(a[:dv], a[dv:dv + 1]) for a in accs])
            return c

        lax.fori_loop(0, s_len // tq_on, qbody, 0)


def _mla_attn_kernel(bounded_ref, q_ref, k_ref, vt_ref, o_ref, *scratch, tk):
    ones = jnp.ones((16, tk), MXU_DTYPE)
    dv = MLA_V

    def qk(row0, width, j, h):
        k = k_ref[0, _rows(j * tk, tk), h * LANES:(h + 1) * LANES]
        return _dot_nt(k, q_ref[0, _rows(row0, width), h * LANES:(h + 1) * LANES])

    def pv(j, h, p, with_ones):
        vt = _vt_tile(vt_ref, j, tk, slice(h * dv, (h + 1) * dv))
        return _dot(jnp.concatenate([vt, ones], axis=0) if with_ones else vt, p)

    def finish(row0, width, acc_l):
        o = jnp.concatenate([a / l for a, l in acc_l], axis=0)
        o_ref[0, _rows(row0, width), :] = o.T.astype(o_ref.dtype)

    _flash_either(bounded_ref, q_ref.shape[1], 2, qk, pv, finish, scratch)


def _mla_attn_call(bounded, qm, km, vmt, tq, tk):
    b, s, _ = qm.shape
    kern = functools.partial(_mla_attn_kernel, tk=tk)
    return pl.pallas_call(
        kern,
        grid=(b, MLA_HEADS // 2),
        in_specs=[
            pl.BlockSpec(memory_space=pltpu.SMEM),
            pl.BlockSpec((1, s, 2 * LANES), lambda bi, hp: (bi, 0, hp)),
            pl.BlockSpec((1, s, 2 * LANES), lambda bi, hp: (bi, 0, hp)),
            pl.BlockSpec((1, vmt.shape[1], 2 * MLA_V, vmt.shape[3]), lambda bi, hp: (bi, 0, hp, 0)),
        ],
        out_specs=pl.BlockSpec((1, s, 2 * MLA_V), lambda bi, hp: (bi, 0, hp)),
        out_shape=jax.ShapeDtypeStruct((b, s, MLA_HEADS * MLA_V), MXU_DTYPE),
        scratch_shapes=_flash_scratch(tq, tk, MLA_V + 16),
        compiler_params=pltpu.CompilerParams(
            dimension_semantics=("arbitrary", "arbitrary"), vmem_limit_bytes=VMEM_LIMIT),
        name="mla_attn",
    )(bounded, qm, km, vmt)


def _diff_attn_kernel(bounded_ref, q_ref, k_ref, vt_ref, lq1_ref, lk1_ref, lq2_ref, lk2_ref,
                      lam0_ref, g_ref, o_ref, *scratch, tk):
    ones = jnp.ones((16, tk), MXU_DTYPE)
    dv = DIFF_DV
    lam0 = lam0_ref[...]
    lam = (jnp.exp(jnp.sum(lq1_ref[...] * lk1_ref[...], axis=-1, keepdims=True))
           - jnp.exp(jnp.sum(lq2_ref[...] * lk2_ref[...], axis=-1, keepdims=True)) + lam0)
    gain = g_ref[...] * (1.0 - lam0)

    def qk(row0, width, j, mi):
        k = k_ref[0, _rows(j * tk, tk), :]
        return _dot_nt(k, q_ref[0, _rows(row0, width), mi * LANES:(mi + 1) * LANES])

    def pv(j, mi, p, with_ones):
        vt = _vt_tile(vt_ref, j, tk, slice(None))
        return _dot(jnp.concatenate([vt, ones], axis=0) if with_ones else vt, p)

    def finish(row0, width, acc_l):
        (a1, l1), (a2, l2) = acc_l
        o = a1 / l1 - lam * (a2 / l2)
        ms = jnp.mean(o * o, axis=0, keepdims=True)
        y = o * lax.rsqrt(ms + EPS) * gain
        o_ref[0, _rows(row0, width), :] = y.T.astype(o_ref.dtype)

    _flash_either(bounded_ref, q_ref.shape[1], 2, qk, pv, finish, scratch)


def _diff_attn_call(bounded, qd, kd, vdt, lp, tq, tk):
    b, s, _ = qd.shape
    kern = functools.partial(_diff_attn_kernel, tk=tk)
    small = [lp['lq1'], lp['lk1'], lp['lq2'], lp['lk2'], lp['lam0'], lp['g_sub']]
    return pl.pallas_call(
        kern,
        grid=(b, DIFF_HEADS),
        in_specs=[
            pl.BlockSpec(memory_space=pltpu.SMEM),
            pl.BlockSpec((1, s, 2 * LANES), lambda bi, h: (bi, 0, h)),
            pl.BlockSpec((1, s, LANES), lambda bi, h: (bi, 0, h)),
            pl.BlockSpec((1, vdt.shape[1], DIFF_DV, vdt.shape[3]), lambda bi, h: (bi, 0, h, 0)),
        ] + [pl.BlockSpec(a.shape, lambda bi, h: (0, 0)) for a in small],
        out_specs=pl.BlockSpec((1, s, DIFF_DV), lambda bi, h: (bi, 0, h)),
        out_shape=jax.ShapeDtypeStruct((b, s, DIFF_HEADS * DIFF_DV), MXU_DTYPE),
        scratch_shapes=_flash_scratch(tq, tk, DIFF_DV + 16),
        compiler_params=pltpu.CompilerParams(
            dimension_semantics=("arbitrary", "arbitrary"), vmem_limit_bytes=VMEM_LIMIT),
        name="diff_attn",
    )(bounded, qd, kd, vdt, *small)


def _ffn_kernel(x_ref, xp_ref, xn_ref, am_ref, amp_ref, amn_ref, ad_ref, adp_ref, adn_ref,
                wom_ref, wod_ref, ln2_ref, wg_ref, wu_ref, cw_ref, wd_ref,
                o_ref, h_scr):
    tm = x_ref.shape[1]
    halo = xp_ref.shape[1]
    ti = pl.program_id(1)
    nt = pl.num_programs(1)
    nchunk = wg_ref.shape[0]

    x_ext = jnp.concatenate([xp_ref[0], x_ref[0], xn_ref[0]], axis=0)
    am_ext = jnp.concatenate([amp_ref[0], am_ref[0], amn_ref[0]], axis=0)
    ad_ext = jnp.concatenate([adp_ref[0], ad_ref[0], adn_ref[0]], axis=0)
    xmid = x_ext + _dot(am_ext, wom_ref[...]) + _dot(ad_ext, wod_ref[...])
    h_scr[...] = _rms(xmid, ln2_ref[...]).astype(h_scr.dtype)
    o_ref[0] = xmid[halo:halo + tm]

    row = lax.broadcasted_iota(jnp.int32, (tm + 2 * halo, 1), 0)
    valid = jnp.logical_and(jnp.logical_or(row >= halo, ti > 0),
                            jnp.logical_or(row < halo + tm, ti < nt - 1))

    n_ext = tm + 2 * halo

    def chunk(c, carry):
        h_ext = h_scr[...]
        g = jnp.where(valid, _dot(h_ext, wg_ref[c]), 0.0)
        u = _dot(h_ext[halo:halo + tm], wu_ref[c])
        cw = cw_ref[c]
        g_prev = pltpu.roll(g, 1, 0)[halo:halo + tm]
        g_next = pltpu.roll(g, n_ext - 1, 0)[halo:halo + tm]
        conv = g_prev * cw[0:1] + g[halo:halo + tm] * cw[1:2] + g_next * cw[2:3] + cw[3:4]
        a = ((0.5 * conv) * (1.0 + jnp.tanh(0.5 * conv)) * u).astype(h_scr.dtype)
        o_ref[0] += _dot(a, wd_ref[c])
        return carry

    lax.fori_loop(0, nchunk, chunk, 0)


def _ffn_call(x, am, ad, lp, tm):
    b, s, d = x.shape
    nt = s // tm
    halo = FFN_HALO
    r = tm // halo
    nhb = s // halo

    def main(w):
        return pl.BlockSpec((1, tm, w), lambda bi, ti: (bi, ti, 0))

    def prev(w):
        return pl.BlockSpec((1, halo, w), lambda bi, ti: (bi, jnp.maximum(ti * r - 1, 0), 0))

    def nxt(w):
        return pl.BlockSpec((1, halo, w), lambda bi, ti: (bi, jnp.minimum((ti + 1) * r, nhb - 1), 0))

    def resident(a):
        return pl.BlockSpec(a.shape, lambda bi, ti: (0,) * a.ndim, pipeline_mode=pl.Buffered(1))

    weights = [lp['wo_m'], lp['wo_d'], lp['ln2'], lp['w_gate'], lp['w_up'], lp['conv'], lp['w_down']]
    wa = am.shape[-1]
    in_specs = ([main(d), prev(d), nxt(d), main(wa), prev(wa), nxt(wa), main(wa), prev(wa), nxt(wa)]
                + [resident(a) for a in weights])
    return pl.pallas_call(
        _ffn_kernel,
        grid=(b, nt),
        in_specs=in_specs,
        out_specs=main(d),
        out_shape=jax.ShapeDtypeStruct((b, s, d), jnp.float32),
        scratch_shapes=[pltpu.VMEM((tm + 2 * halo, d), MXU_DTYPE)],
        compiler_params=pltpu.CompilerParams(
            dimension_semantics=("arbitrary", "arbitrary"), vmem_limit_bytes=VMEM_LIMIT),
        name="ffn",
    )(x, x, x, am, am, am, ad, ad, ad, *weights)


def _rope_tables(s, half, theta, lane_starts):
    d = 2 * half
    freqs = theta ** (-jnp.arange(half, dtype=jnp.float32) * 2.0 / d)
    ang = jnp.arange(s, dtype=jnp.float32)[:, None] * freqs[None, :]
    cos, sin = jnp.cos(ang), jnp.sin(ang)
    cos_t = jnp.ones((s, LANES), jnp.float32)
    sa = jnp.zeros((s, LANES), jnp.float32)
    sb = jnp.zeros((s, LANES), jnp.float32)
    for st in lane_starts:
        cos_t = cos_t.at[:, st:st + half].set(cos).at[:, st + half:st + d].set(cos)
        sa = sa.at[:, st:st + half].set(-sin)
        sb = sb.at[:, st + half:st + d].set(sin)
    return cos_t, sa, sb


def _block_ones(width):
    idx = np.arange(256) // width
    return jnp.asarray(idx[:, None] == idx[None, :], MXU_DTYPE)


def _tables(s):
    cosm, sma, smb = _rope_tables(s, MLA_ROPE // 2, MLA_THETA, (MLA_NOPE,))
    cosd, sda, sdb = _rope_tables(s, DIFF_ROPE // 2, ROPE_THETA, (0, DIFF_DK))
    return dict(cosm=cosm, sma=sma, smb=smb, cosd=cosd, sda=sda, sdb=sdb,
                bo128=_block_ones(LANES), bo64=_block_ones(DIFF_DK))


def _layer_params(l, p):
    f32 = jnp.float32
    cd = MXU_DTYPE
    w_in = p['w_in'][l]
    o_cq, o_ckv, o_kpe = 0, MLA_Q_LORA, MLA_Q_LORA + MLA_KV_LORA
    o_dq = o_kpe + MLA_ROPE
    o_dk = o_dq + DIFF_HEADS * 2 * DIFF_DK
    o_dv = o_dk + DIFF_HEADS * 2 * DIFF_DK
    kpe_pad = jnp.zeros((D_MODEL, LANES), f32).at[:, MLA_NOPE:MLA_QK].set(w_in[:, o_kpe:o_dq])
    w_in_main = jnp.concatenate([w_in[:, o_cq:o_kpe], kpe_pad, w_in[:, o_dq:o_dv]], axis=1)
    w_q = p['w_q_up'][l].reshape(MLA_Q_LORA, MLA_HEADS, MLA_QK)
    w_q = jnp.pad(w_q, ((0, 0), (0, 0), (0, LANES - MLA_QK))).reshape(MLA_Q_LORA, MLA_HEADS * LANES)
    w_kv = p['w_kv_up'][l].reshape(MLA_KV_LORA, MLA_HEADS, MLA_NOPE + MLA_V)
    w_k = jnp.pad(w_kv[:, :, :MLA_NOPE], ((0, 0), (0, 0), (0, LANES - MLA_NOPE)))
    w_k = w_k.reshape(MLA_KV_LORA, MLA_HEADS * LANES)
    w_vt = w_kv[:, :, MLA_NOPE:].reshape(MLA_KV_LORA, MLA_HEADS * MLA_V).T
    nchunk = D_FF // FFN_CHUNK
    conv = jnp.concatenate([p['conv_w'][l], p['conv_b'][l][None, :],
                            jnp.zeros((4, D_FF), f32)], axis=0)
    lam0 = 0.8 - 0.6 * math.exp(-0.3 * l)
    row = lambda v: v.astype(f32)[None, :]

    def score_bounded(gq, gk, width):
        bound = (width ** 0.5 * LOG2E * NORM_SLACK) * jnp.max(jnp.abs(gq)) * jnp.max(jnp.abs(gk))
        return (bound <= SCORE_BOUND_LOG2).astype(jnp.int32).reshape(1)

    w_out = p['w_out'][l]
    return dict(
        mla_bounded=score_bounded(p['mla_qn_g'][l], p['mla_kn_g'][l], MLA_QK),
        diff_bounded=score_bounded(p['diff_qn_g'][l], p['diff_kn_g'][l], DIFF_DK),
        ln1=row(p['ln1_g'][l]),
        w_in=w_in_main.astype(cd),
        w_dvt=w_in[:, o_dv:].T.astype(cd),
        g_cq=row(p['mla_q_norm_g'][l]),
        w_q=w_q.astype(cd),
        g_ckv=row(p['mla_kv_norm_g'][l]),
        w_k=w_k.astype(cd),
        w_vt=w_vt.astype(cd),
        g_qn=row(jnp.pad(p['mla_qn_g'][l], (0, LANES - MLA_QK))),
        g_kn=row(jnp.pad(p['mla_kn_g'][l], (0, LANES - MLA_QK))),
        g_dq=row(jnp.tile(p['diff_qn_g'][l], 2)),
        g_dk=row(jnp.tile(p['diff_kn_g'][l], 2)),
        lq1=row(p['lambda_q1'][l]), lk1=row(p['lambda_k1'][l]),
        lq2=row(p['lambda_q2'][l]), lk2=row(p['lambda_k2'][l]),
        lam0=jnp.full((1, 1), lam0, f32),
        g_sub=p['diff_subln_g'][l].astype(f32)[:, None],
        wo_m=w_out[:MLA_HEADS * MLA_V].astype(cd),
        wo_d=w_out[MLA_HEADS * MLA_V:].astype(cd),
        ln2=row(p['ln2_g'][l]),
        w_gate=p['w_gate'][l].reshape(D_MODEL, nchunk, FFN_CHUNK).transpose(1, 0, 2).astype(cd),
        w_up=p['w_up'][l].reshape(D_MODEL, nchunk, FFN_CHUNK).transpose(1, 0, 2).astype(cd),
        conv=conv.reshape(8, nchunk, FFN_CHUNK).transpose(1, 0, 2),
        w_down=p['w_down'][l].reshape(nchunk, FFN_CHUNK, D_MODEL).astype(cd),
    )


def _trunk(x, layers):
    b, s, _ = x.shape
    tq = min(ATT_TQ, s)
    tk = min(ATT_TK, s)
    tm_pre = min(PRE_TM, s)
    tm_ffn = min(FFN_TM, s)
    tabs = _tables(s)
    for lp in layers:
        qm, km, vmt, qd, kd, vdt = _pre_call(x, lp, tabs, tm_pre, min(VT_TILE, s))
        am = _mla_attn_call(lp['mla_bounded'], qm, km, vmt, tq, tk)
        ad = _diff_attn_call(lp['diff_bounded'], qd, kd, vdt, lp, tq, tk)
        x = _ffn_call(x, am, ad, lp, tm_ffn)
    return x


def kernel(x_prompt, x_sample, ln1_g, w_in, mla_q_norm_g, w_q_up, mla_kv_norm_g, w_kv_up,
           mla_qn_g, mla_kn_g, diff_qn_g, diff_kn_g, lambda_q1, lambda_k1, lambda_q2,
           lambda_k2, diff_subln_g, w_out, ln2_g, w_gate, conv_w, conv_b, w_up, w_down):
    params = dict(ln1_g=ln1_g, w_in=w_in, mla_q_norm_g=mla_q_norm_g, w_q_up=w_q_up,
                  mla_kv_norm_g=mla_kv_norm_g, w_kv_up=w_kv_up, mla_qn_g=mla_qn_g,
                  mla_kn_g=mla_kn_g, diff_qn_g=diff_qn_g, diff_kn_g=diff_kn_g,
                  lambda_q1=lambda_q1, lambda_k1=lambda_k1, lambda_q2=lambda_q2,
                  lambda_k2=lambda_k2, diff_subln_g=diff_subln_g, w_out=w_out, ln2_g=ln2_g,
                  w_gate=w_gate, conv_w=conv_w, conv_b=conv_b, w_up=w_up, w_down=w_down)
    layers = [_layer_params(l, params) for l in range(DEPTH)]
    return (_trunk(x_prompt, layers), _trunk(x_sample, layers))
```

```python
import functools
import math

import jax
import jax.numpy as jnp
import numpy as np
from jax import lax
from jax.experimental import pallas as pl
from jax.experimental.pallas import tpu as pltpu

D_MODEL = 1024
DEPTH = 4
MLA_HEADS = 8
MLA_Q_LORA = 256
MLA_KV_LORA = 128
MLA_NOPE = 64
MLA_ROPE = 32
MLA_V = 64
MLA_QK = MLA_NOPE + MLA_ROPE
MLA_THETA = 10000.0
DIFF_HEADS = 4
DIFF_DK = 64
DIFF_DV = 2 * DIFF_DK
DIFF_ROPE = DIFF_DK // 4
ROPE_THETA = 500000.0
D_FF = 2816
EPS = 1e-6

LANES = 128
MXU_DTYPE = jnp.bfloat16
VMEM_LIMIT = 56 * 1024 * 1024

PRE_TM = 512
ATT_TQ = 2048
ATT_TQ_ONLINE = 1024
ATT_TK = 1024
VT_TILE = 512
ATT_SUB = 256
FFN_TM = 1024
FFN_HALO = 16
FFN_CHUNK = 256
NEG_BIG = -1e30
LOG2E = math.log2(math.e)
SCORE_BOUND_LOG2 = 60.0
NORM_SLACK = 1.05

_NT = (((1,), (1,)), ((), ()))


def _dot(a, b):
    return jnp.dot(a, b, preferred_element_type=jnp.float32)


def _dot_nt(a, b):
    return lax.dot_general(a, b, _NT, preferred_element_type=jnp.float32)


def _rms(x, g):
    ms = jnp.mean(x * x, axis=-1, keepdims=True)
    return x * lax.rsqrt(ms + EPS) * g


def _group_rsqrt(xc, block_ones, width):
    ssq = _dot((xc * xc).astype(MXU_DTYPE), block_ones)
    return lax.rsqrt(ssq * (1.0 / width) + EPS)


def _rope_group(xg, cos, sin_a, sin_b, shift):
    return (xg * cos + pltpu.roll(xg, LANES - shift, 1) * sin_a
            + pltpu.roll(xg, shift, 1) * sin_b)


def _pre_kernel(x_ref, ln1_ref, win_ref, wdvt_ref, gcq_ref, wq_ref, gckv_ref, wk_ref, wvt_ref,
                gqn_ref, gkn_ref, gdq_ref, gdk_ref,
                cosm_ref, sma_ref, smb_ref, cosd_ref, sda_ref, sdb_ref,
                bo128_ref, bo64_ref,
                qm_ref, km_ref, vmt_ref, qd_ref, kd_ref, vdt_ref):
    tm = x_ref.shape[1]
    tk = vmt_ref.shape[3]
    x = x_ref[0]
    hb = _rms(x, ln1_ref[...]).astype(MXU_DTYPE)
    proj = _dot(hb, win_ref[...])
    c_q = proj[:, 0:256]
    c_kv = proj[:, 256:384]
    kpe = proj[:, 384:512]
    dq = proj[:, 512:1024]
    dk = proj[:, 1024:1536]

    dvt = _dot_nt(wdvt_ref[...], hb).astype(MXU_DTYPE)
    cqn = _rms(c_q, gcq_ref[...]).astype(MXU_DTYPE)
    ckvn = _rms(c_kv, gckv_ref[...]).astype(MXU_DTYPE)
    q = _dot(cqn, wq_ref[...])
    kn = _dot(ckvn, wk_ref[...])
    vt = _dot_nt(wvt_ref[...], ckvn).astype(MXU_DTYPE)
    for t in range(tm // tk):
        vmt_ref[0, t] = vt[:, t * tk:(t + 1) * tk]
        vdt_ref[0, t] = dvt[:, t * tk:(t + 1) * tk]

    bo128 = bo128_ref[...]
    bo64 = bo64_ref[...]
    cosm, sma, smb = cosm_ref[...], sma_ref[...], smb_ref[...]
    cosd, sda, sdb = cosd_ref[...], sda_ref[...], sdb_ref[...]
    gqn, gkn, gdq, gdk = gqn_ref[...], gkn_ref[...], gdq_ref[...], gdk_ref[...]
    kpe2 = jnp.concatenate([kpe, kpe], axis=1)
    q_scale = MLA_QK ** -0.5 * LOG2E
    d_scale = DIFF_DK ** -0.5 * LOG2E
    half_m = MLA_ROPE // 2
    half_d = DIFF_ROPE // 2
    lane = lax.broadcasted_iota(jnp.int32, (tm, LANES), 1)

    for c in range(MLA_HEADS // 2):
        sl = slice(c * 256, (c + 1) * 256)
        qc = q[:, sl]
        qc = qc * _group_rsqrt(qc, bo128, MLA_QK)
        kc = kn[:, sl] + kpe2
        kc = kc * _group_rsqrt(kc, bo128, MLA_QK)
        for g in range(2):
            gs = slice(g * LANES, (g + 1) * LANES)
            out = slice((2 * c + g) * LANES, (2 * c + g + 1) * LANES)
            qg = _rope_group(qc[:, gs] * gqn, cosm, sma, smb, half_m)
            qm_ref[0, :, out] = (qg * q_scale).astype(MXU_DTYPE)
            kg = _rope_group(kc[:, gs] * gkn, cosm, sma, smb, half_m)
            km_ref[0, :, out] = kg.astype(MXU_DTYPE)

    for c in range(DIFF_HEADS // 2):
        sl = slice(c * 256, (c + 1) * 256)
        qc = dq[:, sl]
        qc = qc * _group_rsqrt(qc, bo64, DIFF_DK)
        kc = dk[:, sl]
        kc = kc * _group_rsqrt(kc, bo64, DIFF_DK)
        for g in range(2):
            h = 2 * c + g
            gs = slice(g * LANES, (g + 1) * LANES)
            qg = _rope_group(qc[:, gs] * gdq, cosd, sda, sdb, half_d) * d_scale
            zero = jnp.zeros_like(qg)
            qd_ref[0, :, (2 * h) * LANES:(2 * h + 1) * LANES] = (
                jnp.where(lane < DIFF_DK, qg, zero).astype(MXU_DTYPE))
            qd_ref[0, :, (2 * h + 1) * LANES:(2 * h + 2) * LANES] = (
                jnp.where(lane >= DIFF_DK, qg, zero).astype(MXU_DTYPE))
            kg = _rope_group(kc[:, gs] * gdk, cosd, sda, sdb, half_d)
            kd_ref[0, :, h * LANES:(h + 1) * LANES] = kg.astype(MXU_DTYPE)


def _pre_call(x, lp, tabs, tm, tk):
    b, s, d = x.shape
    nt = s // tm
    tpb = tm // tk

    def full(a):
        return pl.BlockSpec(a.shape, lambda bi, ti: (0,) * a.ndim)

    def tab_spec():
        return pl.BlockSpec((tm, LANES), lambda bi, ti: (ti, 0))

    weights = [lp['ln1'], lp['w_in'], lp['w_dvt'], lp['g_cq'], lp['w_q'], lp['g_ckv'], lp['w_k'],
               lp['w_vt'], lp['g_qn'], lp['g_kn'], lp['g_dq'], lp['g_dk']]
    consts = [tabs['bo128'], tabs['bo64']]
    rope = [tabs['cosm'], tabs['sma'], tabs['smb'], tabs['cosd'], tabs['sda'], tabs['sdb']]
    in_specs = ([pl.BlockSpec((1, tm, d), lambda bi, ti: (bi, ti, 0))]
                + [full(a) for a in weights] + [tab_spec() for _ in rope] + [full(a) for a in consts])
    tok = lambda w: pl.BlockSpec((1, tm, w), lambda bi, ti: (bi, ti, 0))
    vts = pl.BlockSpec((1, tpb, 512, tk), lambda bi, ti: (bi, ti, 0, 0))
    out_shape = [
        jax.ShapeDtypeStruct((b, s, 1024), MXU_DTYPE),
        jax.ShapeDtypeStruct((b, s, 1024), MXU_DTYPE),
        jax.ShapeDtypeStruct((b, s // tk, 512, tk), MXU_DTYPE),
        jax.ShapeDtypeStruct((b, s, 1024), MXU_DTYPE),
        jax.ShapeDtypeStruct((b, s, 512), MXU_DTYPE),
        jax.ShapeDtypeStruct((b, s // tk, 512, tk), MXU_DTYPE),
    ]
    out_specs = [tok(1024), tok(1024), vts, tok(1024), tok(512), vts]
    return pl.pallas_call(
        _pre_kernel,
        grid=(b, nt),
        in_specs=in_specs,
        out_specs=out_specs,
        out_shape=out_shape,
        compiler_params=pltpu.CompilerParams(
            dimension_semantics=("arbitrary", "arbitrary"), vmem_limit_bytes=VMEM_LIMIT),
        name="proj",
    )(x, *weights, *rope, *consts)


def _flash_pipelined(nk, n_streams, qk_fn, pv_fn, s_scr, p_scr, tq, dvp):
    for st in range(n_streams):
        s_scr[st] = qk_fn(0, st)
        p_scr[st] = jnp.zeros(p_scr.shape[1:], p_scr.dtype)

    def body(j, carry):
        jn = jnp.minimum(j + 1, nk - 1)
        jp = jnp.maximum(j - 1, 0)
        out = []
        for st in range(n_streams):
            m, alpha, acc = carry[st]
            s_cur = s_scr[st]
            acc = acc * alpha + pv_fn(jp, st, p_scr[st])
            s_nxt = qk_fn(jn, st)
            m_new = jnp.maximum(m, jnp.max(s_cur, axis=0, keepdims=True))
            p_scr[st] = jnp.exp2(s_cur - m_new).astype(p_scr.dtype)
            s_scr[st] = s_nxt
            out.append((m_new, jnp.exp2(m - m_new), acc))
        return tuple(out)

    init = tuple((jnp.full((1, tq), NEG_BIG, jnp.float32), jnp.ones((1, tq), jnp.float32),
                  jnp.zeros((dvp, tq), jnp.float32)) for _ in range(n_streams))
    res = lax.fori_loop(0, nk, body, init)
    return [res[st][2] * res[st][1] + pv_fn(nk - 1, st, p_scr[st]) for st in range(n_streams)]


def _flash_bounded(nq, nk, n_streams, qk_fn, pv_fn, finish_fn, p_scr, acc_scr):
    total = nq * nk
    n_sub, tk = p_scr.shape[1:3]
    tq = n_sub * ATT_SUB
    dv = acc_scr.shape[2] - 16
    for st in range(n_streams):
        p_scr[st] = jnp.zeros(p_scr.shape[1:], p_scr.dtype)
        acc_scr[st] = jnp.zeros(acc_scr.shape[1:], acc_scr.dtype)

    def finish(qi):
        accs = [jnp.concatenate([acc_scr[st, i] for i in range(n_sub)], axis=1)
                for st in range(n_streams)]
        finish_fn(qi * tq, tq, [(a[:dv], jnp.sum(a[dv:dv + 8], axis=0, keepdims=True)) for a in accs])

    def body(t, c):
        tc = jnp.maximum(t - 1, 0)
        jc = tc % nk
        psums = [[] for _ in range(n_streams)]
        for i in range(n_sub):
            for st in range(n_streams):
                acc_scr[st, i, :dv, :] += pv_fn(jc, st, p_scr[st, i], False)
                p = jnp.exp2(qk_fn((t // nk) * tq + i * ATT_SUB, ATT_SUB, t % nk, st))
                p_scr[st, i] = p.astype(p_scr.dtype)
                psums[st].append(jnp.sum(p.reshape(tk // 8, 8, ATT_SUB), axis=0))

        @pl.when(jnp.logical_and(t >= 1, jc == nk - 1))
        def _():
            finish(tc // nk)
            for st in range(n_streams):
                acc_scr[st] = jnp.zeros(acc_scr.shape[1:], acc_scr.dtype)

        for st in range(n_streams):
            for i in range(n_sub):
                acc_scr[st, i, dv:dv + 8, :] += psums[st][i]
        return c

    lax.fori_loop(0, total, body, 0)
    for i in range(n_sub):
        for st in range(n_streams):
            acc_scr[st, i, :dv, :] += pv_fn(nk - 1, st, p_scr[st, i], False)
    finish(nq - 1)


def _rows(row0, width):
    return pl.ds(pl.multiple_of(row0, width), width)


def _vt_tile(vt_ref, j, tk, rows):
    tv = vt_ref.shape[3]
    n = tk // tv
    parts = [vt_ref[0, j * n + i, rows, :] for i in range(n)]
    return parts[0] if n == 1 else jnp.concatenate(parts, axis=1)


def _flash_scratch(tq, tk, dvp, n_streams=2):
    tq_on = min(tq, ATT_TQ_ONLINE)
    n_sub = tq // ATT_SUB
    return [pltpu.VMEM((n_streams, tk, tq_on), jnp.float32),
            pltpu.VMEM((n_streams, tk, tq_on), MXU_DTYPE),
            pltpu.VMEM((n_streams, n_sub, tk, ATT_SUB), MXU_DTYPE),
            pltpu.VMEM((n_streams, n_sub, dvp, ATT_SUB), jnp.float32)]


def _flash_either(bounded_ref, s_len, n_streams, qk_fn, pv_fn, finish_fn, scratch):
    s_scr, p_on_scr, p_scr, acc_scr = scratch
    tk, tq_on = s_scr.shape[1:]
    tq = p_scr.shape[1] * ATT_SUB
    nk = s_len // tk
    dvp = acc_scr.shape[2]
    dv = dvp - 16

    @pl.when(bounded_ref[0] != 0)
    def _():
        _flash_bounded(s_len // tq, nk, n_streams, qk_fn, pv_fn, finish_fn, p_scr, acc_scr)

    @pl.when(bounded_ref[0] == 0)
    def _():
        def qbody(qi, c):
            accs = _flash_pipelined(nk, n_streams, lambda j, st: qk_fn(qi * tq_on, tq_on, j, st),
                                    lambda j, st, p: pv_fn(j, st, p, True),
                                    s_scr, p_on_scr, tq_on, dvp)
            finish_fn(qi * tq_on, tq_on, [(a[:dv], a[dv:dv + 1]) for a in accs])
            return c

        lax.fori_loop(0, s_len // tq_on, qbody, 0)


def _mla_attn_kernel(bounded_ref, q_ref, k_ref, vt_ref, o_ref, *scratch, tk):
    ones = jnp.ones((16, tk), MXU_DTYPE)
    dv = MLA_V

    def qk(row0, width, j, h):
        k = k_ref[0, _rows(j * tk, tk), h * LANES:(h + 1) * LANES]
        return _dot_nt(k, q_ref[0, _rows(row0, width), h * LANES:(h + 1) * LANES])

    def pv(j, h, p, with_ones):
        vt = _vt_tile(vt_ref, j, tk, slice(h * dv, (h + 1) * dv))
        return _dot(jnp.concatenate([vt, ones], axis=0) if with_ones else vt, p)

    def finish(row0, width, acc_l):
        o = jnp.concatenate([a / l for a, l in acc_l], axis=0)
        o_ref[0, _rows(row0, width), :] = o.T.astype(o_ref.dtype)

    _flash_either(bounded_ref, q_ref.shape[1], 2, qk, pv, finish, scratch)


def _mla_attn_call(bounded, qm, km, vmt, tq, tk):
    b, s, _ = qm.shape
    kern = functools.partial(_mla_attn_kernel, tk=tk)
    return pl.pallas_call(
        kern,
        grid=(b, MLA_HEADS // 2),
        in_specs=[
            pl.BlockSpec(memory_space=pltpu.SMEM),
            pl.BlockSpec((1, s, 2 * LANES), lambda bi, hp: (bi, 0, hp)),
            pl.BlockSpec((1, s, 2 * LANES), lambda bi, hp: (bi, 0, hp)),
            pl.BlockSpec((1, vmt.shape[1], 2 * MLA_V, vmt.shape[3]), lambda bi, hp: (bi, 0, hp, 0)),
        ],
        out_specs=pl.BlockSpec((1, s, 2 * MLA_V), lambda bi, hp: (bi, 0, hp)),
        out_shape=jax.ShapeDtypeStruct((b, s, MLA_HEADS * MLA_V), MXU_DTYPE),
        scratch_shapes=_flash_scratch(tq, tk, MLA_V + 16),
        compiler_params=pltpu.CompilerParams(
            dimension_semantics=("arbitrary", "arbitrary"), vmem_limit_bytes=VMEM_LIMIT),
        name="mla_attn",
    )(bounded, qm, km, vmt)


def _diff_attn_kernel(bounded_ref, q_ref, k_ref, vt_ref, lq1_ref, lk1_ref, lq2_ref, lk2_ref,
                      lam0_ref, g_ref, o_ref, *scratch, tk):
    ones = jnp.ones((16, tk), MXU_DTYPE)
    dv = DIFF_DV
    lam0 = lam0_ref[...]
    lam = (jnp.exp(jnp.sum(lq1_ref[...] * lk1_ref[...], axis=-1, keepdims=True))
           - jnp.exp(jnp.sum(lq2_ref[...] * lk2_ref[...], axis=-1, keepdims=True)) + lam0)
    gain = g_ref[...] * (1.0 - lam0)

    def qk(row0, width, j, mi):
        k = k_ref[0, _rows(j * tk, tk), :]
        return _dot_nt(k, q_ref[0, _rows(row0, width), mi * LANES:(mi + 1) * LANES])

    def pv(j, mi, p, with_ones):
        vt = _vt_tile(vt_ref, j, tk, slice(None))
        return _dot(jnp.concatenate([vt, ones], axis=0) if with_ones else vt, p)

    def finish(row0, width, acc_l):
        (a1, l1), (a2, l2) = acc_l
        o = a1 / l1 - lam * (a2 / l2)
        ms = jnp.mean(o * o, axis=0, keepdims=True)
        y = o * lax.rsqrt(ms + EPS) * gain
        o_ref[0, _rows(row0, width), :] = y.T.astype(o_ref.dtype)

    _flash_either(bounded_ref, q_ref.shape[1], 2, qk, pv, finish, scratch)


def _diff_attn_call(bounded, qd, kd, vdt, lp, tq, tk):
    b, s, _ = qd.shape
    kern = functools.partial(_diff_attn_kernel, tk=tk)
    small = [lp['lq1'], lp['lk1'], lp['lq2'], lp['lk2'], lp['lam0'], lp['g_sub']]
    return pl.pallas_call(
        kern,
        grid=(b, DIFF_HEADS),
        in_specs=[
            pl.BlockSpec(memory_space=pltpu.SMEM),
            pl.BlockSpec((1, s, 2 * LANES), lambda bi, h: (bi, 0, h)),
            pl.BlockSpec((1, s, LANES), lambda bi, h: (bi, 0, h)),
            pl.BlockSpec((1, vdt.shape[1], DIFF_DV, vdt.shape[3]), lambda bi, h: (bi, 0, h, 0)),
        ] + [pl.BlockSpec(a.shape, lambda bi, h: (0, 0)) for a in small],
        out_specs=pl.BlockSpec((1, s, DIFF_DV), lambda bi, h: (bi, 0, h)),
        out_shape=jax.ShapeDtypeStruct((b, s, DIFF_HEADS * DIFF_DV), MXU_DTYPE),
        scratch_shapes=_flash_scratch(tq, tk, DIFF_DV + 16),
        compiler_params=pltpu.CompilerParams(
            dimension_semantics=("arbitrary", "arbitrary"), vmem_limit_bytes=VMEM_LIMIT),
        name="diff_attn",
    )(bounded, qd, kd, vdt, *small)


def _ffn_kernel(x_ref, xp_ref, xn_ref, am_ref, amp_ref, amn_ref, ad_ref, adp_ref, adn_ref,
                wom_ref, wod_ref, ln2_ref, wg_ref, wu_ref, cw_ref, wd_ref,
                o_ref, h_scr):
    tm = x_ref.shape[1]
    halo = xp_ref.shape[1]
    ti = pl.program_id(1)
    nt = pl.num_programs(1)
    nchunk = wg_ref.shape[0]

    x_ext = jnp.concatenate([xp_ref[0], x_ref[0], xn_ref[0]], axis=0)
    am_ext = jnp.concatenate([amp_ref[0], am_ref[0], amn_ref[0]], axis=0)
    ad_ext = jnp.concatenate([adp_ref[0], ad_ref[0], adn_ref[0]], axis=0)
    xmid = x_ext + _dot(am_ext, wom_ref[...]) + _dot(ad_ext, wod_ref[...])
    h_scr[...] = _rms(xmid, ln2_ref[...]).astype(h_scr.dtype)
    o_ref[0] = xmid[halo:halo + tm]

    row = lax.broadcasted_iota(jnp.int32, (tm + 2 * halo, 1), 0)
    valid = jnp.logical_and(jnp.logical_or(row >= halo, ti > 0),
                            jnp.logical_or(row < halo + tm, ti < nt - 1))

    n_ext = tm + 2 * halo

    def chunk(c, carry):
        h_ext = h_scr[...]
        g = jnp.where(valid, _dot(h_ext, wg_ref[c]), 0.0)
        u = _dot(h_ext[halo:halo + tm], wu_ref[c])
        cw = cw_ref[c]
        g_prev = pltpu.roll(g, 1, 0)[halo:halo + tm]
        g_next = pltpu.roll(g, n_ext - 1, 0)[halo:halo + tm]
        conv = g_prev * cw[0:1] + g[halo:halo + tm] * cw[1:2] + g_next * cw[2:3] + cw[3:4]
        a = ((0.5 * conv) * (1.0 + jnp.tanh(0.5 * conv)) * u).astype(h_scr.dtype)
        o_ref[0] += _dot(a, wd_ref[c])
        return carry

    lax.fori_loop(0, nchunk, chunk, 0)


def _ffn_call(x, am, ad, lp, tm):
    b, s, d = x.shape
    nt = s // tm
    halo = FFN_HALO
    r = tm // halo
    nhb = s // halo

    def main(w):
        return pl.BlockSpec((1, tm, w), lambda bi, ti: (bi, ti, 0))

    def prev(w):
        return pl.BlockSpec((1, halo, w), lambda bi, ti: (bi, jnp.maximum(ti * r - 1, 0), 0))

    def nxt(w):
        return pl.BlockSpec((1, halo, w), lambda bi, ti: (bi, jnp.minimum((ti + 1) * r, nhb - 1), 0))

    def resident(a):
        return pl.BlockSpec(a.shape, lambda bi, ti: (0,) * a.ndim, pipeline_mode=pl.Buffered(1))

    weights = [lp['wo_m'], lp['wo_d'], lp['ln2'], lp['w_gate'], lp['w_up'], lp['conv'], lp['w_down']]
    wa = am.shape[-1]
    in_specs = ([main(d), prev(d), nxt(d), main(wa), prev(wa), nxt(wa), main(wa), prev(wa), nxt(wa)]
                + [resident(a) for a in weights])
    return pl.pallas_call(
        _ffn_kernel,
        grid=(b, nt),
        in_specs=in_specs,
        out_specs=main(d),
        out_shape=jax.ShapeDtypeStruct((b, s, d), jnp.float32),
        scratch_shapes=[pltpu.VMEM((tm + 2 * halo, d), MXU_DTYPE)],
        compiler_params=pltpu.CompilerParams(
            dimension_semantics=("arbitrary", "arbitrary"), vmem_limit_bytes=VMEM_LIMIT),
        name="ffn",
    )(x, x, x, am, am, am, ad, ad, ad, *weights)


def _rope_tables(s, half, theta, lane_starts):
    d = 2 * half
    freqs = theta ** (-jnp.arange(half, dtype=jnp.float32) * 2.0 / d)
    idx = np.zeros(LANES, np.int32)
    role = np.zeros(LANES, np.int32)
    for st in lane_starts:
        idx[st:st + d] = np.arange(d) % half
        role[st:st + half] = 1
        role[st + half:st + d] = 2
    ang = jnp.arange(s, dtype=jnp.float32)[:, None] * freqs[idx][None, :]
    cos, sin = jnp.cos(ang), jnp.sin(ang)
    role = jnp.asarray(role)[None, :]
    return (jnp.where(role > 0, cos, 1.0), jnp.where(role == 1, -sin, 0.0),
            jnp.where(role == 2, sin, 0.0))


def _block_ones(width):
    idx = np.arange(256) // width
    return jnp.asarray(idx[:, None] == idx[None, :], MXU_DTYPE)


def _tables(s):
    cosm, sma, smb = _rope_tables(s, MLA_ROPE // 2, MLA_THETA, (MLA_NOPE,))
    cosd, sda, sdb = _rope_tables(s, DIFF_ROPE // 2, ROPE_THETA, (0, DIFF_DK))
    return dict(cosm=cosm, sma=sma, smb=smb, cosd=cosd, sda=sda, sdb=sdb,
                bo128=_block_ones(LANES), bo64=_block_ones(DIFF_DK))


def _layer_params(l, p):
    f32 = jnp.float32
    cd = MXU_DTYPE
    w_in = p['w_in'][l]
    o_cq, o_ckv, o_kpe = 0, MLA_Q_LORA, MLA_Q_LORA + MLA_KV_LORA
    o_dq = o_kpe + MLA_ROPE
    o_dk = o_dq + DIFF_HEADS * 2 * DIFF_DK
    o_dv = o_dk + DIFF_HEADS * 2 * DIFF_DK
    kpe_pad = jnp.pad(w_in[:, o_kpe:o_dq], ((0, 0), (MLA_NOPE, LANES - MLA_QK)))
    w_in_main = jnp.concatenate([w_in[:, o_cq:o_kpe], kpe_pad, w_in[:, o_dq:o_dv]], axis=1)
    w_q = p['w_q_up'][l].reshape(MLA_Q_LORA, MLA_HEADS, MLA_QK)
    w_q = jnp.pad(w_q, ((0, 0), (0, 0), (0, LANES - MLA_QK))).reshape(MLA_Q_LORA, MLA_HEADS * LANES)
    w_kv = p['w_kv_up'][l].reshape(MLA_KV_LORA, MLA_HEADS, MLA_NOPE + MLA_V)
    w_k = jnp.pad(w_kv[:, :, :MLA_NOPE], ((0, 0), (0, 0), (0, LANES - MLA_NOPE)))
    w_k = w_k.reshape(MLA_KV_LORA, MLA_HEADS * LANES)
    w_vt = w_kv[:, :, MLA_NOPE:].reshape(MLA_KV_LORA, MLA_HEADS * MLA_V).T
    nchunk = D_FF // FFN_CHUNK
    conv = jnp.concatenate([p['conv_w'][l], p['conv_b'][l][None, :],
                            jnp.zeros((4, D_FF), f32)], axis=0)
    lam0 = 0.8 - 0.6 * math.exp(-0.3 * l)
    row = lambda v: v.astype(f32)[None, :]

    def score_bounded(gq, gk, width):
        bound = (width ** 0.5 * LOG2E * NORM_SLACK) * jnp.max(jnp.abs(gq)) * jnp.max(jnp.abs(gk))
        return (bound <= SCORE_BOUND_LOG2).astype(jnp.int32).reshape(1)

    w_out = p['w_out'][l]
    return dict(
        mla_bounded=score_bounded(p['mla_qn_g'][l], p['mla_kn_g'][l], MLA_QK),
        diff_bounded=score_bounded(p['diff_qn_g'][l], p['diff_kn_g'][l], DIFF_DK),
        ln1=row(p['ln1_g'][l]),
        w_in=w_in_main.astype(cd),
        w_dvt=w_in[:, o_dv:].T.astype(cd),
        g_cq=row(p['mla_q_norm_g'][l]),
        w_q=w_q.astype(cd),
        g_ckv=row(p['mla_kv_norm_g'][l]),
        w_k=w_k.astype(cd),
        w_vt=w_vt.astype(cd),
        g_qn=row(jnp.pad(p['mla_qn_g'][l], (0, LANES - MLA_QK))),
        g_kn=row(jnp.pad(p['mla_kn_g'][l], (0, LANES - MLA_QK))),
        g_dq=row(jnp.tile(p['diff_qn_g'][l], 2)),
        g_dk=row(jnp.tile(p['diff_kn_g'][l], 2)),
        lq1=row(p['lambda_q1'][l]), lk1=row(p['lambda_k1'][l]),
        lq2=row(p['lambda_q2'][l]), lk2=row(p['lambda_k2'][l]),
        lam0=jnp.full((1, 1), lam0, f32),
        g_sub=p['diff_subln_g'][l].astype(f32)[:, None],
        wo_m=w_out[:MLA_HEADS * MLA_V].astype(cd),
        wo_d=w_out[MLA_HEADS * MLA_V:].astype(cd),
        ln2=row(p['ln2_g'][l]),
        w_gate=p['w_gate'][l].reshape(D_MODEL, nchunk, FFN_CHUNK).transpose(1, 0, 2).astype(cd),
        w_up=p['w_up'][l].reshape(D_MODEL, nchunk, FFN_CHUNK).transpose(1, 0, 2).astype(cd),
        conv=conv.reshape(8, nchunk, FFN_CHUNK).transpose(1, 0, 2),
        w_down=p['w_down'][l].reshape(nchunk, FFN_CHUNK, D_MODEL).astype(cd),
    )


def _trunk(x, layers, tabs):
    b, s, _ = x.shape
    tq = min(ATT_TQ, s)
    tk = min(ATT_TK, s)
    tm_pre = min(PRE_TM, s)
    tm_ffn = min(FFN_TM, s)
    for lp in layers:
        qm, km, vmt, qd, kd, vdt = _pre_call(x, lp, tabs, tm_pre, min(VT_TILE, s))
        am = _mla_attn_call(lp['mla_bounded'], qm, km, vmt, tq, tk)
        ad = _diff_attn_call(lp['diff_bounded'], qd, kd, vdt, lp, tq, tk)
        x = _ffn_call(x, am, ad, lp, tm_ffn)
    return x


def kernel(x_prompt, x_sample, ln1_g, w_in, mla_q_norm_g, w_q_up, mla_kv_norm_g, w_kv_up,
           mla_qn_g, mla_kn_g, diff_qn_g, diff_kn_g, lambda_q1, lambda_k1, lambda_q2,
           lambda_k2, diff_subln_g, w_out, ln2_g, w_gate, conv_w, conv_b, w_up, w_down):
    params = dict(ln1_g=ln1_g, w_in=w_in, mla_q_norm_g=mla_q_norm_g, w_q_up=w_q_up,
                  mla_kv_norm_g=mla_kv_norm_g, w_kv_up=w_kv_up, mla_qn_g=mla_qn_g,
                  mla_kn_g=mla_kn_g, diff_qn_g=diff_qn_g, diff_kn_g=diff_kn_g,
                  lambda_q1=lambda_q1, lambda_k1=lambda_k1, lambda_q2=lambda_q2,
                  lambda_k2=lambda_k2, diff_subln_g=diff_subln_g, w_out=w_out, ln2_g=ln2_g,
                  w_gate=w_gate, conv_w=conv_w, conv_b=conv_b, w_up=w_up, w_down=w_down)
    layers = [_layer_params(l, params) for l in range(DEPTH)]
    tabs = {x.shape[1]: _tables(x.shape[1]) for x in (x_prompt, x_sample)}
    return tuple(_trunk(x, layers, tabs[x.shape[1]]) for x in (x_prompt, x_sample))
```

```python
import functools
import math

import jax
import jax.numpy as jnp
import numpy as np
from jax import lax
from jax.experimental import pallas as pl
from jax.experimental.pallas import tpu as pltpu

D_MODEL = 1024
DEPTH = 4
MLA_HEADS = 8
MLA_Q_LORA = 256
MLA_KV_LORA = 128
MLA_NOPE = 64
MLA_ROPE = 32
MLA_V = 64
MLA_QK = MLA_NOPE + MLA_ROPE
MLA_THETA = 10000.0
DIFF_HEADS = 4
DIFF_DK = 64
DIFF_DV = 2 * DIFF_DK
DIFF_ROPE = DIFF_DK // 4
ROPE_THETA = 500000.0
D_FF = 2816
EPS = 1e-6

LANES = 128
MXU_DTYPE = jnp.bfloat16
VMEM_LIMIT = 56 * 1024 * 1024

PRE_TM = 512
ATT_TQ = 2048
ATT_TQ_ONLINE = 1024
ATT_TK = 1024
VT_TILE = 512
ATT_SUB = 256
FFN_TM = 1024
FFN_HALO = 16
FFN_CHUNK = 256
NEG_BIG = -1e30
LOG2E = math.log2(math.e)
SCORE_BOUND_LOG2 = 60.0
NORM_SLACK = 1.05

_NT = (((1,), (1,)), ((), ()))


def _dot(a, b):
    return jnp.dot(a, b, preferred_element_type=jnp.float32)


def _dot_nt(a, b):
    return lax.dot_general(a, b, _NT, preferred_element_type=jnp.float32)


def _rms(x, g):
    ms = jnp.mean(x * x, axis=-1, keepdims=True)
    return x * lax.rsqrt(ms + EPS) * g


def _group_rsqrt(xc, block_ones, width):
    ssq = _dot((xc * xc).astype(MXU_DTYPE), block_ones)
    return lax.rsqrt(ssq * (1.0 / width) + EPS)


def _rope_group(xg, cos, sin_a, sin_b, shift):
    return (xg * cos + pltpu.roll(xg, LANES - shift, 1) * sin_a
            + pltpu.roll(xg, shift, 1) * sin_b)


def _pre_kernel(x_ref, ln1_ref, win_ref, wdvt_ref, gcq_ref, wq_ref, gckv_ref, wk_ref, wvt_ref,
                gqn_ref, gkn_ref, gdq_ref, gdk_ref,
                cosm_ref, sma_ref, smb_ref, cosd_ref, sda_ref, sdb_ref,
                bo128_ref, bo64_ref,
                qm_ref, km_ref, vmt_ref, qd_ref, kd_ref, vdt_ref):
    tm = x_ref.shape[1]
    tk = vmt_ref.shape[3]
    x = x_ref[0]
    hb = _rms(x, ln1_ref[...]).astype(MXU_DTYPE)
    proj = _dot(hb, win_ref[...])
    c_q = proj[:, 0:256]
    c_kv = proj[:, 256:384]
    kpe = proj[:, 384:512]
    dq = proj[:, 512:1024]
    dk = proj[:, 1024:1536]

    dvt = _dot_nt(wdvt_ref[...], hb).astype(MXU_DTYPE)
    cqn = _rms(c_q, gcq_ref[...]).astype(MXU_DTYPE)
    ckvn = _rms(c_kv, gckv_ref[...]).astype(MXU_DTYPE)
    q = _dot(cqn, wq_ref[...])
    kn = _dot(ckvn, wk_ref[...])
    vt = _dot_nt(wvt_ref[...], ckvn).astype(MXU_DTYPE)
    for t in range(tm // tk):
        vmt_ref[0, t] = vt[:, t * tk:(t + 1) * tk]
        vdt_ref[0, t] = dvt[:, t * tk:(t + 1) * tk]

    bo128 = bo128_ref[...]
    bo64 = bo64_ref[...]
    cosm, sma, smb = cosm_ref[...], sma_ref[...], smb_ref[...]
    cosd, sda, sdb = cosd_ref[...], sda_ref[...], sdb_ref[...]
    gqn, gkn, gdq, gdk = gqn_ref[...], gkn_ref[...], gdq_ref[...], gdk_ref[...]
    kpe2 = jnp.concatenate([kpe, kpe], axis=1)
    q_scale = MLA_QK ** -0.5 * LOG2E
    d_scale = DIFF_DK ** -0.5 * LOG2E
    half_m = MLA_ROPE // 2
    half_d = DIFF_ROPE // 2
    lane = lax.broadcasted_iota(jnp.int32, (tm, LANES), 1)

    for c in range(MLA_HEADS // 2):
        sl = slice(c * 256, (c + 1) * 256)
        qc = q[:, sl]
        qc = qc * _group_rsqrt(qc, bo128, MLA_QK)
        kc = kn[:, sl] + kpe2
        kc = kc * _group_rsqrt(kc, bo128, MLA_QK)
        for g in range(2):
            gs = slice(g * LANES, (g + 1) * LANES)
            out = slice((2 * c + g) * LANES, (2 * c + g + 1) * LANES)
            qg = _rope_group(qc[:, gs] * gqn, cosm, sma, smb, half_m)
            qm_ref[0, :, out] = (qg * q_scale).astype(MXU_DTYPE)
            kg = _rope_group(kc[:, gs] * gkn, cosm, sma, smb, half_m)
            km_ref[0, :, out] = kg.astype(MXU_DTYPE)

    for c in range(DIFF_HEADS // 2):
        sl = slice(c * 256, (c + 1) * 256)
        qc = dq[:, sl]
        qc = qc * _group_rsqrt(qc, bo64, DIFF_DK)
        kc = dk[:, sl]
        kc = kc * _group_rsqrt(kc, bo64, DIFF_DK)
        for g in range(2):
            h = 2 * c + g
            gs = slice(g * LANES, (g + 1) * LANES)
            qg = _rope_group(qc[:, gs] * gdq, cosd, sda, sdb, half_d) * d_scale
            zero = jnp.zeros_like(qg)
            qd_ref[0, :, (2 * h) * LANES:(2 * h + 1) * LANES] = (
                jnp.where(lane < DIFF_DK, qg, zero).astype(MXU_DTYPE))
            qd_ref[0, :, (2 * h + 1) * LANES:(2 * h + 2) * LANES] = (
                jnp.where(lane >= DIFF_DK, qg, zero).astype(MXU_DTYPE))
            kg = _rope_group(kc[:, gs] * gdk, cosd, sda, sdb, half_d)
            kd_ref[0, :, h * LANES:(h + 1) * LANES] = kg.astype(MXU_DTYPE)


def _pre_call(x, lp, tabs, tm, tk):
    b, s, d = x.shape
    nt = s // tm
    tpb = tm // tk

    def full(a):
        return pl.BlockSpec(a.shape, lambda bi, ti: (0,) * a.ndim)

    def tab_spec():
        return pl.BlockSpec((tm, LANES), lambda bi, ti: (ti, 0))

    weights = [lp['ln1'], lp['w_in'], lp['w_dvt'], lp['g_cq'], lp['w_q'], lp['g_ckv'], lp['w_k'],
               lp['w_vt'], lp['g_qn'], lp['g_kn'], lp['g_dq'], lp['g_dk']]
    consts = [tabs['bo128'], tabs['bo64']]
    rope = [tabs['cosm'], tabs['sma'], tabs['smb'], tabs['cosd'], tabs['sda'], tabs['sdb']]
    in_specs = ([pl.BlockSpec((1, tm, d), lambda bi, ti: (bi, ti, 0))]
                + [full(a) for a in weights] + [tab_spec() for _ in rope] + [full(a) for a in consts])
    tok = lambda w: pl.BlockSpec((1, tm, w), lambda bi, ti: (bi, ti, 0))
    vts = pl.BlockSpec((1, tpb, 512, tk), lambda bi, ti: (bi, ti, 0, 0))
    out_shape = [
        jax.ShapeDtypeStruct((b, s, 1024), MXU_DTYPE),
        jax.ShapeDtypeStruct((b, s, 1024), MXU_DTYPE),
        jax.ShapeDtypeStruct((b, s // tk, 512, tk), MXU_DTYPE),
        jax.ShapeDtypeStruct((b, s, 1024), MXU_DTYPE),
        jax.ShapeDtypeStruct((b, s, 512), MXU_DTYPE),
        jax.ShapeDtypeStruct((b, s // tk, 512, tk), MXU_DTYPE),
    ]
    out_specs = [tok(1024), tok(1024), vts, tok(1024), tok(512), vts]
    return pl.pallas_call(
        _pre_kernel,
        grid=(b, nt),
        in_specs=in_specs,
        out_specs=out_specs,
        out_shape=out_shape,
        compiler_params=pltpu.CompilerParams(
            dimension_semantics=("arbitrary", "arbitrary"), vmem_limit_bytes=VMEM_LIMIT),
        name="proj",
    )(x, *weights, *rope, *consts)


def _flash_pipelined(nk, n_streams, qk_fn, pv_fn, s_scr, p_scr, tq, dvp):
    for st in range(n_streams):
        s_scr[st] = qk_fn(0, st)
        p_scr[st] = jnp.zeros(p_scr.shape[1:], p_scr.dtype)

    def body(j, carry):
        jn = jnp.minimum(j + 1, nk - 1)
        jp = jnp.maximum(j - 1, 0)
        out = []
        for st in range(n_streams):
            m, alpha, acc = carry[st]
            s_cur = s_scr[st]
            acc = acc * alpha + pv_fn(jp, st, p_scr[st])
            s_nxt = qk_fn(jn, st)
            m_new = jnp.maximum(m, jnp.max(s_cur, axis=0, keepdims=True))
            p_scr[st] = jnp.exp2(s_cur - m_new).astype(p_scr.dtype)
            s_scr[st] = s_nxt
            out.append((m_new, jnp.exp2(m - m_new), acc))
        return tuple(out)

    init = tuple((jnp.full((1, tq), NEG_BIG, jnp.float32), jnp.ones((1, tq), jnp.float32),
                  jnp.zeros((dvp, tq), jnp.float32)) for _ in range(n_streams))
    res = lax.fori_loop(0, nk, body, init)
    return [res[st][2] * res[st][1] + pv_fn(nk - 1, st, p_scr[st]) for st in range(n_streams)]


def _flash_bounded(nq, nk, n_streams, qk_fn, pv_fn, finish_fn, p_scr, acc_scr):
    total = nq * nk
    n_sub, tk = p_scr.shape[1:3]
    tq = n_sub * ATT_SUB
    dv = acc_scr.shape[2] - 16
    for st in range(n_streams):
        acc_scr[st] = jnp.zeros(acc_scr.shape[1:], acc_scr.dtype)

    def finish(qi):
        accs = [jnp.concatenate([acc_scr[st, i] for i in range(n_sub)], axis=1)
                for st in range(n_streams)]
        finish_fn(qi * tq, tq, [(a[:dv], jnp.sum(a[dv:dv + 8], axis=0, keepdims=True)) for a in accs])

    def pv_stage(j, i, st):
        acc_scr[st, i, :dv, :] += pv_fn(j, st, p_scr[st, i], False)

    def score_stage(t, i, st):
        p = jnp.exp2(qk_fn((t // nk) * tq + i * ATT_SUB, ATT_SUB, t % nk, st))
        p_scr[st, i] = p.astype(p_scr.dtype)
        return jnp.sum(p.reshape(tk // 8, 8, ATT_SUB), axis=0)

    def add_row_sums(psums):
        for st in range(n_streams):
            for i in range(n_sub):
                acc_scr[st, i, dv:dv + 8, :] += psums[st][i]

    add_row_sums([[score_stage(0, i, st) for i in range(n_sub)] for st in range(n_streams)])

    def body(t, c):
        jc = (t - 1) % nk
        psums = [[None] * n_sub for _ in range(n_streams)]
        for i in range(n_sub):
            for st in range(n_streams):
                pv_stage(jc, i, st)
                psums[st][i] = score_stage(t, i, st)

        @pl.when(jc == nk - 1)
        def _():
            finish((t - 1) // nk)
            for st in range(n_streams):
                acc_scr[st] = jnp.zeros(acc_scr.shape[1:], acc_scr.dtype)

        add_row_sums(psums)
        return c

    lax.fori_loop(1, total, body, 0)
    for i in range(n_sub):
        for st in range(n_streams):
            pv_stage(nk - 1, i, st)
    finish(nq - 1)


def _rows(row0, width):
    return pl.ds(pl.multiple_of(row0, width), width)


def _vt_tile(vt_ref, j, tk, rows):
    tv = vt_ref.shape[3]
    n = tk // tv
    parts = [vt_ref[0, j * n + i, rows, :] for i in range(n)]
    return parts[0] if n == 1 else jnp.concatenate(parts, axis=1)


def _flash_scratch(tq, tk, dvp, n_streams=2):
    tq_on = min(tq, ATT_TQ_ONLINE)
    n_sub = tq // ATT_SUB
    return [pltpu.VMEM((n_streams, tk, tq_on), jnp.float32),
            pltpu.VMEM((n_streams, tk, tq_on), MXU_DTYPE),
            pltpu.VMEM((n_streams, n_sub, tk, ATT_SUB), MXU_DTYPE),
            pltpu.VMEM((n_streams, n_sub, dvp, ATT_SUB), jnp.float32)]


def _flash_either(bounded_ref, s_len, n_streams, qk_fn, pv_fn, finish_fn, scratch):
    s_scr, p_on_scr, p_scr, acc_scr = scratch
    tk, tq_on = s_scr.shape[1:]
    tq = p_scr.shape[1] * ATT_SUB
    nk = s_len // tk
    dvp = acc_scr.shape[2]
    dv = dvp - 16

    @pl.when(bounded_ref[0] != 0)
    def _():
        _flash_bounded(s_len // tq, nk, n_streams, qk_fn, pv_fn, finish_fn, p_scr, acc_scr)

    @pl.when(bounded_ref[0] == 0)
    def _():
        def qbody(qi, c):
            accs = _flash_pipelined(nk, n_streams, lambda j, st: qk_fn(qi * tq_on, tq_on, j, st),
                                    lambda j, st, p: pv_fn(j, st, p, True),
                                    s_scr, p_on_scr, tq_on, dvp)
            finish_fn(qi * tq_on, tq_on, [(a[:dv], a[dv:dv + 1]) for a in accs])
            return c

        lax.fori_loop(0, s_len // tq_on, qbody, 0)


def _mla_attn_kernel(bounded_ref, q_ref, k_ref, vt_ref, o_ref, *scratch, tk):
    ones = jnp.ones((16, tk), MXU_DTYPE)
    dv = MLA_V

    def qk(row0, width, j, h):
        k = k_ref[0, _rows(j * tk, tk), h * LANES:(h + 1) * LANES]
        return _dot_nt(k, q_ref[0, _rows(row0, width), h * LANES:(h + 1) * LANES])

    def pv(j, h, p, with_ones):
        vt = _vt_tile(vt_ref, j, tk, slice(h * dv, (h + 1) * dv))
        return _dot(jnp.concatenate([vt, ones], axis=0) if with_ones else vt, p)

    def finish(row0, width, acc_l):
        o = jnp.concatenate([a / l for a, l in acc_l], axis=0)
        o_ref[0, _rows(row0, width), :] = o.T.astype(o_ref.dtype)

    _flash_either(bounded_ref, q_ref.shape[1], 2, qk, pv, finish, scratch)


def _mla_attn_call(bounded, qm, km, vmt, tq, tk):
    b, s, _ = qm.shape
    kern = functools.partial(_mla_attn_kernel, tk=tk)
    return pl.pallas_call(
        kern,
        grid=(b, MLA_HEADS // 2),
        in_specs=[
            pl.BlockSpec(memory_space=pltpu.SMEM),
            pl.BlockSpec((1, s, 2 * LANES), lambda bi, hp: (bi, 0, hp)),
            pl.BlockSpec((1, s, 2 * LANES), lambda bi, hp: (bi, 0, hp)),
            pl.BlockSpec((1, vmt.shape[1], 2 * MLA_V, vmt.shape[3]), lambda bi, hp: (bi, 0, hp, 0)),
        ],
        out_specs=pl.BlockSpec((1, s, 2 * MLA_V), lambda bi, hp: (bi, 0, hp)),
        out_shape=jax.ShapeDtypeStruct((b, s, MLA_HEADS * MLA_V), MXU_DTYPE),
        scratch_shapes=_flash_scratch(tq, tk, MLA_V + 16),
        compiler_params=pltpu.CompilerParams(
            dimension_semantics=("arbitrary", "arbitrary"), vmem_limit_bytes=VMEM_LIMIT),
        name="mla_attn",
    )(bounded, qm, km, vmt)


def _diff_attn_kernel(bounded_ref, q_ref, k_ref, vt_ref, lq1_ref, lk1_ref, lq2_ref, lk2_ref,
                      lam0_ref, g_ref, o_ref, *scratch, tk):
    ones = jnp.ones((16, tk), MXU_DTYPE)
    dv = DIFF_DV
    lam0 = lam0_ref[...]
    lam = (jnp.exp(jnp.sum(lq1_ref[...] * lk1_ref[...], axis=-1, keepdims=True))
           - jnp.exp(jnp.sum(lq2_ref[...] * lk2_ref[...], axis=-1, keepdims=True)) + lam0)
    gain = g_ref[...] * (1.0 - lam0)

    def qk(row0, width, j, mi):
        k = k_ref[0, _rows(j * tk, tk), :]
        return _dot_nt(k, q_ref[0, _rows(row0, width), mi * LANES:(mi + 1) * LANES])

    def pv(j, mi, p, with_ones):
        vt = _vt_tile(vt_ref, j, tk, slice(None))
        return _dot(jnp.concatenate([vt, ones], axis=0) if with_ones else vt, p)

    def finish(row0, width, acc_l):
        (a1, l1), (a2, l2) = acc_l
        o = a1 / l1 - lam * (a2 / l2)
        ms = jnp.mean(o * o, axis=0, keepdims=True)
        y = o * lax.rsqrt(ms + EPS) * gain
        o_ref[0, _rows(row0, width), :] = y.T.astype(o_ref.dtype)

    _flash_either(bounded_ref, q_ref.shape[1], 2, qk, pv, finish, scratch)


def _diff_attn_call(bounded, qd, kd, vdt, lp, tq, tk):
    b, s, _ = qd.shape
    kern = functools.partial(_diff_attn_kernel, tk=tk)
    small = [lp['lq1'], lp['lk1'], lp['lq2'], lp['lk2'], lp['lam0'], lp['g_sub']]
    return pl.pallas_call(
        kern,
        grid=(b, DIFF_HEADS),
        in_specs=[
            pl.BlockSpec(memory_space=pltpu.SMEM),
            pl.BlockSpec((1, s, 2 * LANES), lambda bi, h: (bi, 0, h)),
            pl.BlockSpec((1, s, LANES), lambda bi, h: (bi, 0, h)),
            pl.BlockSpec((1, vdt.shape[1], DIFF_DV, vdt.shape[3]), lambda bi, h: (bi, 0, h, 0)),
        ] + [pl.BlockSpec(a.shape, lambda bi, h: (0, 0)) for a in small],
        out_specs=pl.BlockSpec((1, s, DIFF_DV), lambda bi, h: (bi, 0, h)),
        out_shape=jax.ShapeDtypeStruct((b, s, DIFF_HEADS * DIFF_DV), MXU_DTYPE),
        scratch_shapes=_flash_scratch(tq, tk, DIFF_DV + 16),
        compiler_params=pltpu.CompilerParams(
            dimension_semantics=("arbitrary", "arbitrary"), vmem_limit_bytes=VMEM_LIMIT),
        name="diff_attn",
    )(bounded, qd, kd, vdt, *small)


def _ffn_kernel(x_ref, xp_ref, xn_ref, am_ref, amp_ref, amn_ref, ad_ref, adp_ref, adn_ref,
                wom_ref, wod_ref, ln2_ref, wg_ref, wu_ref, cw_ref, wd_ref,
                o_ref, h_scr):
    tm = x_ref.shape[1]
    halo = xp_ref.shape[1]
    ti = pl.program_id(1)
    nt = pl.num_programs(1)
    nchunk = wg_ref.shape[0]

    x_ext = jnp.concatenate([xp_ref[0], x_ref[0], xn_ref[0]], axis=0)
    am_ext = jnp.concatenate([amp_ref[0], am_ref[0], amn_ref[0]], axis=0)
    ad_ext = jnp.concatenate([adp_ref[0], ad_ref[0], adn_ref[0]], axis=0)
    xmid = x_ext + _dot(am_ext, wom_ref[...]) + _dot(ad_ext, wod_ref[...])
    h_scr[...] = _rms(xmid, ln2_ref[...]).astype(h_scr.dtype)
    o_ref[0] = xmid[halo:halo + tm]

    row = lax.broadcasted_iota(jnp.int32, (tm + 2 * halo, 1), 0)
    valid = jnp.logical_and(jnp.logical_or(row >= halo, ti > 0),
                            jnp.logical_or(row < halo + tm, ti < nt - 1))

    n_ext = tm + 2 * halo

    def chunk(c, carry):
        h_ext = h_scr[...]
        g = jnp.where(valid, _dot(h_ext, wg_ref[c]), 0.0)
        u = _dot(h_ext[halo:halo + tm], wu_ref[c])
        cw = cw_ref[c]
        g_prev = pltpu.roll(g, 1, 0)[halo:halo + tm]
        g_next = pltpu.roll(g, n_ext - 1, 0)[halo:halo + tm]
        conv = g_prev * cw[0:1] + g[halo:halo + tm] * cw[1:2] + g_next * cw[2:3] + cw[3:4]
        a = ((0.5 * conv) * (1.0 + jnp.tanh(0.5 * conv)) * u).astype(h_scr.dtype)
        o_ref[0] += _dot(a, wd_ref[c])
        return carry

    lax.fori_loop(0, nchunk, chunk, 0)


def _ffn_call(x, am, ad, lp, tm):
    b, s, d = x.shape
    nt = s // tm
    halo = FFN_HALO
    r = tm // halo
    nhb = s // halo

    def main(w):
        return pl.BlockSpec((1, tm, w), lambda bi, ti: (bi, ti, 0))

    def prev(w):
        return pl.BlockSpec((1, halo, w), lambda bi, ti: (bi, jnp.maximum(ti * r - 1, 0), 0))

    def nxt(w):
        return pl.BlockSpec((1, halo, w), lambda bi, ti: (bi, jnp.minimum((ti + 1) * r, nhb - 1), 0))

    def resident(a):
        return pl.BlockSpec(a.shape, lambda bi, ti: (0,) * a.ndim, pipeline_mode=pl.Buffered(1))

    weights = [lp['wo_m'], lp['wo_d'], lp['ln2'], lp['w_gate'], lp['w_up'], lp['conv'], lp['w_down']]
    wa = am.shape[-1]
    in_specs = ([main(d), prev(d), nxt(d), main(wa), prev(wa), nxt(wa), main(wa), prev(wa), nxt(wa)]
                + [resident(a) for a in weights])
    return pl.pallas_call(
        _ffn_kernel,
        grid=(b, nt),
        in_specs=in_specs,
        out_specs=main(d),
        out_shape=jax.ShapeDtypeStruct((b, s, d), jnp.float32),
        scratch_shapes=[pltpu.VMEM((tm + 2 * halo, d), MXU_DTYPE)],
        compiler_params=pltpu.CompilerParams(
            dimension_semantics=("arbitrary", "arbitrary"), vmem_limit_bytes=VMEM_LIMIT),
        name="ffn",
    )(x, x, x, am, am, am, ad, ad, ad, *weights)


def _rope_tables(s, half, theta, lane_starts):
    d = 2 * half
    freqs = theta ** (-jnp.arange(half, dtype=jnp.float32) * 2.0 / d)
    idx = np.zeros(LANES, np.int32)
    role = np.zeros(LANES, np.int32)
    for st in lane_starts:
        idx[st:st + d] = np.arange(d) % half
        role[st:st + half] = 1
        role[st + half:st + d] = 2
    ang = jnp.arange(s, dtype=jnp.float32)[:, None] * freqs[idx][None, :]
    cos, sin = jnp.cos(ang), jnp.sin(ang)
    role = jnp.asarray(role)[None, :]
    return (jnp.where(role > 0, cos, 1.0), jnp.where(role == 1, -sin, 0.0),
            jnp.where(role == 2, sin, 0.0))


def _block_ones(width):
    idx = np.arange(256) // width
    return jnp.asarray(idx[:, None] == idx[None, :], MXU_DTYPE)


def _tables(s):
    cosm, sma, smb = _rope_tables(s, MLA_ROPE // 2, MLA_THETA, (MLA_NOPE,))
    cosd, sda, sdb = _rope_tables(s, DIFF_ROPE // 2, ROPE_THETA, (0, DIFF_DK))
    return dict(cosm=cosm, sma=sma, smb=smb, cosd=cosd, sda=sda, sdb=sdb,
                bo128=_block_ones(LANES), bo64=_block_ones(DIFF_DK))


def _layer_params(l, p):
    f32 = jnp.float32
    cd = MXU_DTYPE
    w_in = p['w_in'][l]
    o_cq, o_ckv, o_kpe = 0, MLA_Q_LORA, MLA_Q_LORA + MLA_KV_LORA
    o_dq = o_kpe + MLA_ROPE
    o_dk = o_dq + DIFF_HEADS * 2 * DIFF_DK
    o_dv = o_dk + DIFF_HEADS * 2 * DIFF_DK
    kpe_pad = jnp.pad(w_in[:, o_kpe:o_dq], ((0, 0), (MLA_NOPE, LANES - MLA_QK)))
    w_in_main = jnp.concatenate([w_in[:, o_cq:o_kpe], kpe_pad, w_in[:, o_dq:o_dv]], axis=1)
    w_q = p['w_q_up'][l].reshape(MLA_Q_LORA, MLA_HEADS, MLA_QK)
    w_q = jnp.pad(w_q, ((0, 0), (0, 0), (0, LANES - MLA_QK))).reshape(MLA_Q_LORA, MLA_HEADS * LANES)
    w_kv = p['w_kv_up'][l].reshape(MLA_KV_LORA, MLA_HEADS, MLA_NOPE + MLA_V)
    w_k = jnp.pad(w_kv[:, :, :MLA_NOPE], ((0, 0), (0, 0), (0, LANES - MLA_NOPE)))
    w_k = w_k.reshape(MLA_KV_LORA, MLA_HEADS * LANES)
    w_vt = w_kv[:, :, MLA_NOPE:].reshape(MLA_KV_LORA, MLA_HEADS * MLA_V).T
    nchunk = D_FF // FFN_CHUNK
    conv = jnp.concatenate([p['conv_w'][l], p['conv_b'][l][None, :],
                            jnp.zeros((4, D_FF), f32)], axis=0)
    lam0 = 0.8 - 0.6 * math.exp(-0.3 * l)
    row = lambda v: v.astype(f32)[None, :]

    def score_bounded(gq, gk, width):
        bound = (width ** 0.5 * LOG2E * NORM_SLACK) * jnp.max(jnp.abs(gq)) * jnp.max(jnp.abs(gk))
        return (bound <= SCORE_BOUND_LOG2).astype(jnp.int32).reshape(1)

    w_out = p['w_out'][l]
    return dict(
        mla_bounded=score_bounded(p['mla_qn_g'][l], p['mla_kn_g'][l], MLA_QK),
        diff_bounded=score_bounded(p['diff_qn_g'][l], p['diff_kn_g'][l], DIFF_DK),
        ln1=row(p['ln1_g'][l]),
        w_in=w_in_main.astype(cd),
        w_dvt=w_in[:, o_dv:].T.astype(cd),
        g_cq=row(p['mla_q_norm_g'][l]),
        w_q=w_q.astype(cd),
        g_ckv=row(p['mla_kv_norm_g'][l]),
        w_k=w_k.astype(cd),
        w_vt=w_vt.astype(cd),
        g_qn=row(jnp.pad(p['mla_qn_g'][l], (0, LANES - MLA_QK))),
        g_kn=row(jnp.pad(p['mla_kn_g'][l], (0, LANES - MLA_QK))),
        g_dq=row(jnp.tile(p['diff_qn_g'][l], 2)),
        g_dk=row(jnp.tile(p['diff_kn_g'][l], 2)),
        lq1=row(p['lambda_q1'][l]), lk1=row(p['lambda_k1'][l]),
        lq2=row(p['lambda_q2'][l]), lk2=row(p['lambda_k2'][l]),
        lam0=jnp.full((1, 1), lam0, f32),
        g_sub=p['diff_subln_g'][l].astype(f32)[:, None],
        wo_m=w_out[:MLA_HEADS * MLA_V].astype(cd),
        wo_d=w_out[MLA_HEADS * MLA_V:].astype(cd),
        ln2=row(p['ln2_g'][l]),
        w_gate=p['w_gate'][l].reshape(D_MODEL, nchunk, FFN_CHUNK).transpose(1, 0, 2).astype(cd),
        w_up=p['w_up'][l].reshape(D_MODEL, nchunk, FFN_CHUNK).transpose(1, 0, 2).astype(cd),
        conv=conv.reshape(8, nchunk, FFN_CHUNK).transpose(1, 0, 2),
        w_down=p['w_down'][l].reshape(nchunk, FFN_CHUNK, D_MODEL).astype(cd),
    )


def _trunk(x, layers, tabs):
    b, s, _ = x.shape
    tq = min(ATT_TQ, s)
    tk = min(ATT_TK, s)
    tm_pre = min(PRE_TM, s)
    tm_ffn = min(FFN_TM, s)
    for lp in layers:
        qm, km, vmt, qd, kd, vdt = _pre_call(x, lp, tabs, tm_pre, min(VT_TILE, s))
        am = _mla_attn_call(lp['mla_bounded'], qm, km, vmt, tq, tk)
        ad = _diff_attn_call(lp['diff_bounded'], qd, kd, vdt, lp, tq, tk)
        x = _ffn_call(x, am, ad, lp, tm_ffn)
    return x


def kernel(x_prompt, x_sample, ln1_g, w_in, mla_q_norm_g, w_q_up, mla_kv_norm_g, w_kv_up,
           mla_qn_g, mla_kn_g, diff_qn_g, diff_kn_g, lambda_q1, lambda_k1, lambda_q2,
           lambda_k2, diff_subln_g, w_out, ln2_g, w_gate, conv_w, conv_b, w_up, w_down):
    params = dict(ln1_g=ln1_g, w_in=w_in, mla_q_norm_g=mla_q_norm_g, w_q_up=w_q_up,
                  mla_kv_norm_g=mla_kv_norm_g, w_kv_up=w_kv_up, mla_qn_g=mla_qn_g,
                  mla_kn_g=mla_kn_g, diff_qn_g=diff_qn_g, diff_kn_g=diff_kn_g,
                  lambda_q1=lambda_q1, lambda_k1=lambda_k1, lambda_q2=lambda_q2,
                  lambda_k2=lambda_k2, diff_subln_g=diff_subln_g, w_out=w_out, ln2_g=ln2_g,
                  w_gate=w_gate, conv_w=conv_w, conv_b=conv_b, w_up=w_up, w_down=w_down)
    layers = [_layer_params(l, params) for l in range(DEPTH)]
    tabs = {x.shape[1]: _tables(x.shape[1]) for x in (x_prompt, x_sample)}
    return tuple(_trunk(x, layers, tabs[x.shape[1]]) for x in (x_prompt, x_sample))
```

```python
import functools
import math

import jax
import jax.numpy as jnp
import numpy as np
from jax import lax
from jax.experimental import pallas as pl
from jax.experimental.pallas import tpu as pltpu

D_MODEL = 1024
DEPTH = 4
MLA_HEADS = 8
MLA_Q_LORA = 256
MLA_KV_LORA = 128
MLA_NOPE = 64
MLA_ROPE = 32
MLA_V = 64
MLA_QK = MLA_NOPE + MLA_ROPE
MLA_THETA = 10000.0
DIFF_HEADS = 4
DIFF_DK = 64
DIFF_DV = 2 * DIFF_DK
DIFF_ROPE = DIFF_DK // 4
ROPE_THETA = 500000.0
D_FF = 2816
EPS = 1e-6

LANES = 128
MXU_DTYPE = jnp.bfloat16
VMEM_LIMIT = 56 * 1024 * 1024

PRE_TM = 512
ATT_TQ = 2048
ATT_TQ_ONLINE = 1024
ATT_TK = 1024
VT_TILE = 512
ATT_SUB = 256
FFN_TM = 1024
FFN_HALO = 16
FFN_CHUNK = 256
NEG_BIG = -1e30
LOG2E = math.log2(math.e)
SCORE_BOUND_LOG2 = 60.0
NORM_SLACK = 1.05

_NT = (((1,), (1,)), ((), ()))


def _dot(a, b):
    return jnp.dot(a, b, preferred_element_type=jnp.float32)


def _dot_nt(a, b):
    return lax.dot_general(a, b, _NT, preferred_element_type=jnp.float32)


def _rms(x, g):
    ms = jnp.mean(x * x, axis=-1, keepdims=True)
    return x * lax.rsqrt(ms + EPS) * g


def _group_rsqrt(xc, block_ones, width):
    ssq = _dot((xc * xc).astype(MXU_DTYPE), block_ones)
    return lax.rsqrt(ssq * (1.0 / width) + EPS)


def _rope_group(xg, cos, sin_a, sin_b, shift):
    return (xg * cos + pltpu.roll(xg, LANES - shift, 1) * sin_a
            + pltpu.roll(xg, shift, 1) * sin_b)


def _pre_kernel(x_ref, ln1_ref, win_ref, wdvt_ref, gcq_ref, wq_ref, gckv_ref, wk_ref, wvt_ref,
                gqn_ref, gkn_ref, gdq_ref, gdk_ref,
                cosm_ref, sma_ref, smb_ref, cosd_ref, sda_ref, sdb_ref,
                bo128_ref, bo64_ref,
                qm_ref, km_ref, vmt_ref, qd_ref, kd_ref, vdt_ref):
    tm = x_ref.shape[1]
    tk = vmt_ref.shape[3]
    x = x_ref[0]
    hb = _rms(x, ln1_ref[...]).astype(MXU_DTYPE)
    proj = _dot(hb, win_ref[...])
    c_q = proj[:, 0:256]
    c_kv = proj[:, 256:384]
    kpe = proj[:, 384:512]
    dq = proj[:, 512:1024]
    dk = proj[:, 1024:1536]

    dvt = _dot_nt(wdvt_ref[...], hb).astype(MXU_DTYPE)
    cqn = _rms(c_q, gcq_ref[...]).astype(MXU_DTYPE)
    ckvn = _rms(c_kv, gckv_ref[...]).astype(MXU_DTYPE)
    q = _dot(cqn, wq_ref[...])
    kn = _dot(ckvn, wk_ref[...])
    vt = _dot_nt(wvt_ref[...], ckvn).astype(MXU_DTYPE)
    for t in range(tm // tk):
        vmt_ref[0, t] = vt[:, t * tk:(t + 1) * tk]
        vdt_ref[0, t] = dvt[:, t * tk:(t + 1) * tk]

    bo128 = bo128_ref[...]
    bo64 = bo64_ref[...]
    cosm, sma, smb = cosm_ref[...], sma_ref[...], smb_ref[...]
    cosd, sda, sdb = cosd_ref[...], sda_ref[...], sdb_ref[...]
    gqn, gkn, gdq, gdk = gqn_ref[...], gkn_ref[...], gdq_ref[...], gdk_ref[...]
    kpe2 = jnp.concatenate([kpe, kpe], axis=1)
    q_scale = MLA_QK ** -0.5 * LOG2E
    d_scale = DIFF_DK ** -0.5 * LOG2E
    half_m = MLA_ROPE // 2
    half_d = DIFF_ROPE // 2
    lane = lax.broadcasted_iota(jnp.int32, (tm, LANES), 1)

    for c in range(MLA_HEADS // 2):
        sl = slice(c * 256, (c + 1) * 256)
        qc = q[:, sl]
        qc = qc * _group_rsqrt(qc, bo128, MLA_QK)
        kc = kn[:, sl] + kpe2
        kc = kc * _group_rsqrt(kc, bo128, MLA_QK)
        for g in range(2):
            gs = slice(g * LANES, (g + 1) * LANES)
            out = slice((2 * c + g) * LANES, (2 * c + g + 1) * LANES)
            qg = _rope_group(qc[:, gs] * gqn, cosm, sma, smb, half_m)
            qm_ref[0, :, out] = (qg * q_scale).astype(MXU_DTYPE)
            kg = _rope_group(kc[:, gs] * gkn, cosm, sma, smb, half_m)
            km_ref[0, :, out] = kg.astype(MXU_DTYPE)

    for c in range(DIFF_HEADS // 2):
        sl = slice(c * 256, (c + 1) * 256)
        qc = dq[:, sl]
        qc = qc * _group_rsqrt(qc, bo64, DIFF_DK)
        kc = dk[:, sl]
        kc = kc * _group_rsqrt(kc, bo64, DIFF_DK)
        for g in range(2):
            h = 2 * c + g
            gs = slice(g * LANES, (g + 1) * LANES)
            qg = _rope_group(qc[:, gs] * gdq, cosd, sda, sdb, half_d) * d_scale
            zero = jnp.zeros_like(qg)
            qd_ref[0, :, (2 * h) * LANES:(2 * h + 1) * LANES] = (
                jnp.where(lane < DIFF_DK, qg, zero).astype(MXU_DTYPE))
            qd_ref[0, :, (2 * h + 1) * LANES:(2 * h + 2) * LANES] = (
                jnp.where(lane >= DIFF_DK, qg, zero).astype(MXU_DTYPE))
            kg = _rope_group(kc[:, gs] * gdk, cosd, sda, sdb, half_d)
            kd_ref[0, :, h * LANES:(h + 1) * LANES] = kg.astype(MXU_DTYPE)


def _pre_call(x, lp, tabs, tm, tk):
    b, s, d = x.shape
    nt = s // tm
    tpb = tm // tk

    def full(a):
        return pl.BlockSpec(a.shape, lambda bi, ti: (0,) * a.ndim)

    def tab_spec():
        return pl.BlockSpec((tm, LANES), lambda bi, ti: (ti, 0))

    weights = [lp['ln1'], lp['w_in'], lp['w_dvt'], lp['g_cq'], lp['w_q'], lp['g_ckv'], lp['w_k'],
               lp['w_vt'], lp['g_qn'], lp['g_kn'], lp['g_dq'], lp['g_dk']]
    consts = [tabs['bo128'], tabs['bo64']]
    rope = [tabs['cosm'], tabs['sma'], tabs['smb'], tabs['cosd'], tabs['sda'], tabs['sdb']]
    in_specs = ([pl.BlockSpec((1, tm, d), lambda bi, ti: (bi, ti, 0))]
                + [full(a) for a in weights] + [tab_spec() for _ in rope] + [full(a) for a in consts])
    tok = lambda w: pl.BlockSpec((1, tm, w), lambda bi, ti: (bi, ti, 0))
    vts = pl.BlockSpec((1, tpb, 512, tk), lambda bi, ti: (bi, ti, 0, 0))
    out_shape = [
        jax.ShapeDtypeStruct((b, s, 1024), MXU_DTYPE),
        jax.ShapeDtypeStruct((b, s, 1024), MXU_DTYPE),
        jax.ShapeDtypeStruct((b, s // tk, 512, tk), MXU_DTYPE),
        jax.ShapeDtypeStruct((b, s, 1024), MXU_DTYPE),
        jax.ShapeDtypeStruct((b, s, 512), MXU_DTYPE),
        jax.ShapeDtypeStruct((b, s // tk, 512, tk), MXU_DTYPE),
    ]
    out_specs = [tok(1024), tok(1024), vts, tok(1024), tok(512), vts]
    return pl.pallas_call(
        _pre_kernel,
        grid=(b, nt),
        in_specs=in_specs,
        out_specs=out_specs,
        out_shape=out_shape,
        compiler_params=pltpu.CompilerParams(
            dimension_semantics=("arbitrary", "arbitrary"), vmem_limit_bytes=VMEM_LIMIT),
        name="proj",
    )(x, *weights, *rope, *consts)


def _flash_pipelined(nk, n_streams, qk_fn, pv_fn, s_scr, p_scr, tq, dvp):
    for st in range(n_streams):
        s_scr[st] = qk_fn(0, st)
        p_scr[st] = jnp.zeros(p_scr.shape[1:], p_scr.dtype)

    def body(j, carry):
        jn = jnp.minimum(j + 1, nk - 1)
        jp = jnp.maximum(j - 1, 0)
        out = []
        for st in range(n_streams):
            m, alpha, acc = carry[st]
            s_cur = s_scr[st]
            acc = acc * alpha + pv_fn(jp, st, p_scr[st])
            s_nxt = qk_fn(jn, st)
            m_new = jnp.maximum(m, jnp.max(s_cur, axis=0, keepdims=True))
            p_scr[st] = jnp.exp2(s_cur - m_new).astype(p_scr.dtype)
            s_scr[st] = s_nxt
            out.append((m_new, jnp.exp2(m - m_new), acc))
        return tuple(out)

    init = tuple((jnp.full((1, tq), NEG_BIG, jnp.float32), jnp.ones((1, tq), jnp.float32),
                  jnp.zeros((dvp, tq), jnp.float32)) for _ in range(n_streams))
    res = lax.fori_loop(0, nk, body, init)
    return [res[st][2] * res[st][1] + pv_fn(nk - 1, st, p_scr[st]) for st in range(n_streams)]


def _flash_bounded(nq, nk, n_streams, qk_fn, pv_fn, finish_fn, p_scr, acc_scr, qt_scr):
    total = nq * nk
    n_sub, tk = p_scr.shape[1:3]
    tq = n_sub * ATT_SUB
    dv = acc_scr.shape[2] - 16
    for st in range(n_streams):
        acc_scr[st] = jnp.zeros(acc_scr.shape[1:], acc_scr.dtype)

    def finish(qi):
        accs = [jnp.concatenate([acc_scr[st, i] for i in range(n_sub)], axis=1)
                for st in range(n_streams)]
        finish_fn(qi * tq, tq, [(a[:dv], jnp.sum(a[dv:dv + 8], axis=0, keepdims=True)) for a in accs])

    def pv_stage(j, i, st):
        acc_scr[st, i, :dv, :] += pv_fn(j, st, p_scr[st, i], False)

    def score_stage(t, i, st):
        if qt_scr is None:
            s = qk_fn((t // nk) * tq + i * ATT_SUB, ATT_SUB, t % nk, st)
        else:
            s = qk_fn(None, None, t % nk, st, qt_scr[st, i])
        p = jnp.exp2(s)
        p_scr[st, i] = p.astype(p_scr.dtype)
        return jnp.sum(p.reshape(tk // 8, 8, ATT_SUB), axis=0)

    def stage_queries(qi):
        for st in range(n_streams):
            for i in range(n_sub):
                q = qk_fn(qi * tq + i * ATT_SUB, ATT_SUB, None, st)
                qt_scr[st, i] = q.astype(jnp.float32).T.astype(qt_scr.dtype)

    def add_row_sums(psums):
        for st in range(n_streams):
            for i in range(n_sub):
                acc_scr[st, i, dv:dv + 8, :] += psums[st][i]

    if qt_scr is not None:
        stage_queries(0)
    add_row_sums([[score_stage(0, i, st) for i in range(n_sub)] for st in range(n_streams)])

    def body(t, c):
        jc = (t - 1) % nk
        psums = [[None] * n_sub for _ in range(n_streams)]
        for i in range(n_sub):
            for st in range(n_streams):
                pv_stage(jc, i, st)
                psums[st][i] = score_stage(t, i, st)

        @pl.when(jc == nk - 1)
        def _():
            finish((t - 1) // nk)
            for st in range(n_streams):
                acc_scr[st] = jnp.zeros(acc_scr.shape[1:], acc_scr.dtype)

        add_row_sums(psums)

        if qt_scr is not None:
            @pl.when(jnp.logical_and((t + 1) % nk == 0, t + 1 < total))
            def _():
                stage_queries((t + 1) // nk)

        return c

    lax.fori_loop(1, total, body, 0)
    for i in range(n_sub):
        for st in range(n_streams):
            pv_stage(nk - 1, i, st)
    finish(nq - 1)


def _rows(row0, width):
    return pl.ds(pl.multiple_of(row0, width), width)


def _vt_tile(vt_ref, j, tk, rows):
    tv = vt_ref.shape[3]
    n = tk // tv
    parts = [vt_ref[0, j * n + i, rows, :] for i in range(n)]
    return parts[0] if n == 1 else jnp.concatenate(parts, axis=1)


def _flash_scratch(tq, tk, dvp, stage_queries, n_streams=2):
    tq_on = min(tq, ATT_TQ_ONLINE)
    n_sub = tq // ATT_SUB
    scratch = [pltpu.VMEM((n_streams, tk, tq_on), jnp.float32),
               pltpu.VMEM((n_streams, tk, tq_on), MXU_DTYPE),
               pltpu.VMEM((n_streams, n_sub, tk, ATT_SUB), MXU_DTYPE),
               pltpu.VMEM((n_streams, n_sub, dvp, ATT_SUB), jnp.float32)]
    if stage_queries:
        scratch.append(pltpu.VMEM((n_streams, n_sub, LANES, ATT_SUB), MXU_DTYPE))
    return scratch


def _flash_either(bounded_ref, s_len, n_streams, qk_fn, pv_fn, finish_fn, scratch):
    s_scr, p_on_scr, p_scr, acc_scr = scratch[:4]
    qt_scr = scratch[4] if len(scratch) > 4 else None
    tk, tq_on = s_scr.shape[1:]
    tq = p_scr.shape[1] * ATT_SUB
    nk = s_len // tk
    dvp = acc_scr.shape[2]
    dv = dvp - 16

    @pl.when(bounded_ref[0] != 0)
    def _():
        _flash_bounded(s_len // tq, nk, n_streams, qk_fn, pv_fn, finish_fn, p_scr, acc_scr, qt_scr)

    @pl.when(bounded_ref[0] == 0)
    def _():
        def qbody(qi, c):
            accs = _flash_pipelined(nk, n_streams, lambda j, st: qk_fn(qi * tq_on, tq_on, j, st),
                                    lambda j, st, p: pv_fn(j, st, p, True),
                                    s_scr, p_on_scr, tq_on, dvp)
            finish_fn(qi * tq_on, tq_on, [(a[:dv], a[dv:dv + 1]) for a in accs])
            return c

        lax.fori_loop(0, s_len // tq_on, qbody, 0)


def _mla_attn_kernel(bounded_ref, q_ref, k_ref, vt_ref, o_ref, *scratch, tk):
    ones = jnp.ones((16, tk), MXU_DTYPE)
    dv = MLA_V

    def qk(row0, width, j, h, q_t=None):
        if j is None:
            return q_ref[0, _rows(row0, width), h * LANES:(h + 1) * LANES]
        k = k_ref[0, _rows(j * tk, tk), h * LANES:(h + 1) * LANES]
        if q_t is not None:
            return _dot(k, q_t)
        return _dot_nt(k, q_ref[0, _rows(row0, width), h * LANES:(h + 1) * LANES])

    def pv(j, h, p, with_ones):
        vt = _vt_tile(vt_ref, j, tk, slice(h * dv, (h + 1) * dv))
        return _dot(jnp.concatenate([vt, ones], axis=0) if with_ones else vt, p)

    def finish(row0, width, acc_l):
        o = jnp.concatenate([a / l for a, l in acc_l], axis=0)
        o_ref[0, _rows(row0, width), :] = o.T.astype(o_ref.dtype)

    _flash_either(bounded_ref, q_ref.shape[1], 2, qk, pv, finish, scratch)


def _mla_attn_call(bounded, qm, km, vmt, tq, tk):
    b, s, _ = qm.shape
    kern = functools.partial(_mla_attn_kernel, tk=tk)
    return pl.pallas_call(
        kern,
        grid=(b, MLA_HEADS // 2),
        in_specs=[
            pl.BlockSpec(memory_space=pltpu.SMEM),
            pl.BlockSpec((1, s, 2 * LANES), lambda bi, hp: (bi, 0, hp)),
            pl.BlockSpec((1, s, 2 * LANES), lambda bi, hp: (bi, 0, hp)),
            pl.BlockSpec((1, vmt.shape[1], 2 * MLA_V, vmt.shape[3]), lambda bi, hp: (bi, 0, hp, 0)),
        ],
        out_specs=pl.BlockSpec((1, s, 2 * MLA_V), lambda bi, hp: (bi, 0, hp)),
        out_shape=jax.ShapeDtypeStruct((b, s, MLA_HEADS * MLA_V), MXU_DTYPE),
        scratch_shapes=_flash_scratch(tq, tk, MLA_V + 16, stage_queries=True),
        compiler_params=pltpu.CompilerParams(
            dimension_semantics=("arbitrary", "arbitrary"), vmem_limit_bytes=VMEM_LIMIT),
        name="mla_attn",
    )(bounded, qm, km, vmt)


def _diff_attn_kernel(bounded_ref, q_ref, k_ref, vt_ref, lq1_ref, lk1_ref, lq2_ref, lk2_ref,
                      lam0_ref, g_ref, o_ref, *scratch, tk):
    ones = jnp.ones((16, tk), MXU_DTYPE)
    dv = DIFF_DV
    lam0 = lam0_ref[...]
    lam = (jnp.exp(jnp.sum(lq1_ref[...] * lk1_ref[...], axis=-1, keepdims=True))
           - jnp.exp(jnp.sum(lq2_ref[...] * lk2_ref[...], axis=-1, keepdims=True)) + lam0)
    gain = g_ref[...] * (1.0 - lam0)

    def qk(row0, width, j, mi):
        k = k_ref[0, _rows(j * tk, tk), :]
        return _dot_nt(k, q_ref[0, _rows(row0, width), mi * LANES:(mi + 1) * LANES])

    def pv(j, mi, p, with_ones):
        vt = _vt_tile(vt_ref, j, tk, slice(None))
        return _dot(jnp.concatenate([vt, ones], axis=0) if with_ones else vt, p)

    def finish(row0, width, acc_l):
        (a1, l1), (a2, l2) = acc_l
        o = a1 / l1 - lam * (a2 / l2)
        ms = jnp.mean(o * o, axis=0, keepdims=True)
        y = o * lax.rsqrt(ms + EPS) * gain
        o_ref[0, _rows(row0, width), :] = y.T.astype(o_ref.dtype)

    _flash_either(bounded_ref, q_ref.shape[1], 2, qk, pv, finish, scratch)


def _diff_attn_call(bounded, qd, kd, vdt, lp, tq, tk):
    b, s, _ = qd.shape
    kern = functools.partial(_diff_attn_kernel, tk=tk)
    small = [lp['lq1'], lp['lk1'], lp['lq2'], lp['lk2'], lp['lam0'], lp['g_sub']]
    return pl.pallas_call(
        kern,
        grid=(b, DIFF_HEADS),
        in_specs=[
            pl.BlockSpec(memory_space=pltpu.SMEM),
            pl.BlockSpec((1, s, 2 * LANES), lambda bi, h: (bi, 0, h)),
            pl.BlockSpec((1, s, LANES), lambda bi, h: (bi, 0, h)),
            pl.BlockSpec((1, vdt.shape[1], DIFF_DV, vdt.shape[3]), lambda bi, h: (bi, 0, h, 0)),
        ] + [pl.BlockSpec(a.shape, lambda bi, h: (0, 0)) for a in small],
        out_specs=pl.BlockSpec((1, s, DIFF_DV), lambda bi, h: (bi, 0, h)),
        out_shape=jax.ShapeDtypeStruct((b, s, DIFF_HEADS * DIFF_DV), MXU_DTYPE),
        scratch_shapes=_flash_scratch(tq, tk, DIFF_DV + 16, stage_queries=False),
        compiler_params=pltpu.CompilerParams(
            dimension_semantics=("arbitrary", "arbitrary"), vmem_limit_bytes=VMEM_LIMIT),
        name="diff_attn",
    )(bounded, qd, kd, vdt, *small)


def _ffn_kernel(x_ref, xp_ref, xn_ref, am_ref, amp_ref, amn_ref, ad_ref, adp_ref, adn_ref,
                wom_ref, wod_ref, ln2_ref, wg_ref, wu_ref, cw_ref, wd_ref,
                o_ref, h_scr):
    tm = x_ref.shape[1]
    halo = xp_ref.shape[1]
    ti = pl.program_id(1)
    nt = pl.num_programs(1)
    nchunk = wg_ref.shape[0]

    x_ext = jnp.concatenate([xp_ref[0], x_ref[0], xn_ref[0]], axis=0)
    am_ext = jnp.concatenate([amp_ref[0], am_ref[0], amn_ref[0]], axis=0)
    ad_ext = jnp.concatenate([adp_ref[0], ad_ref[0], adn_ref[0]], axis=0)
    xmid = x_ext + _dot(am_ext, wom_ref[...]) + _dot(ad_ext, wod_ref[...])
    h_scr[...] = _rms(xmid, ln2_ref[...]).astype(h_scr.dtype)
    o_ref[0] = xmid[halo:halo + tm]

    row = lax.broadcasted_iota(jnp.int32, (tm + 2 * halo, 1), 0)
    valid = jnp.logical_and(jnp.logical_or(row >= halo, ti > 0),
                            jnp.logical_or(row < halo + tm, ti < nt - 1))

    n_ext = tm + 2 * halo

    def chunk(c, carry):
        h_ext = h_scr[...]
        g = jnp.where(valid, _dot(h_ext, wg_ref[c]), 0.0)
        u = _dot(h_ext[halo:halo + tm], wu_ref[c])
        cw = cw_ref[c]
        g_prev = pltpu.roll(g, 1, 0)[halo:halo + tm]
        g_next = pltpu.roll(g, n_ext - 1, 0)[halo:halo + tm]
        conv = g_prev * cw[0:1] + g[halo:halo + tm] * cw[1:2] + g_next * cw[2:3] + cw[3:4]
        a = ((0.5 * conv) * (1.0 + jnp.tanh(0.5 * conv)) * u).astype(h_scr.dtype)
        o_ref[0] += _dot(a, wd_ref[c])
        return carry

    lax.fori_loop(0, nchunk, chunk, 0)


def _ffn_call(x, am, ad, lp, tm):
    b, s, d = x.shape
    nt = s // tm
    halo = FFN_HALO
    r = tm // halo
    nhb = s // halo

    def main(w):
        return pl.BlockSpec((1, tm, w), lambda bi, ti: (bi, ti, 0))

    def prev(w):
        return pl.BlockSpec((1, halo, w), lambda bi, ti: (bi, jnp.maximum(ti * r - 1, 0), 0))

    def nxt(w):
        return pl.BlockSpec((1, halo, w), lambda bi, ti: (bi, jnp.minimum((ti + 1) * r, nhb - 1), 0))

    def resident(a):
        return pl.BlockSpec(a.shape, lambda bi, ti: (0,) * a.ndim, pipeline_mode=pl.Buffered(1))

    weights = [lp['wo_m'], lp['wo_d'], lp['ln2'], lp['w_gate'], lp['w_up'], lp['conv'], lp['w_down']]
    wa = am.shape[-1]
    in_specs = ([main(d), prev(d), nxt(d), main(wa), prev(wa), nxt(wa), main(wa), prev(wa), nxt(wa)]
                + [resident(a) for a in weights])
    return pl.pallas_call(
        _ffn_kernel,
        grid=(b, nt),
        in_specs=in_specs,
        out_specs=main(d),
        out_shape=jax.ShapeDtypeStruct((b, s, d), jnp.float32),
        scratch_shapes=[pltpu.VMEM((tm + 2 * halo, d), MXU_DTYPE)],
        compiler_params=pltpu.CompilerParams(
            dimension_semantics=("arbitrary", "arbitrary"), vmem_limit_bytes=VMEM_LIMIT),
        name="ffn",
    )(x, x, x, am, am, am, ad, ad, ad, *weights)


def _rope_tables(s, half, theta, lane_starts):
    d = 2 * half
    freqs = theta ** (-jnp.arange(half, dtype=jnp.float32) * 2.0 / d)
    idx = np.zeros(LANES, np.int32)
    role = np.zeros(LANES, np.int32)
    for st in lane_starts:
        idx[st:st + d] = np.arange(d) % half
        role[st:st + half] = 1
        role[st + half:st + d] = 2
    ang = jnp.arange(s, dtype=jnp.float32)[:, None] * freqs[idx][None, :]
    cos, sin = jnp.cos(ang), jnp.sin(ang)
    role = jnp.asarray(role)[None, :]
    return (jnp.where(role > 0, cos, 1.0), jnp.where(role == 1, -sin, 0.0),
            jnp.where(role == 2, sin, 0.0))


def _block_ones(width):
    idx = np.arange(256) // width
    return jnp.asarray(idx[:, None] == idx[None, :], MXU_DTYPE)


def _tables(s):
    cosm, sma, smb = _rope_tables(s, MLA_ROPE // 2, MLA_THETA, (MLA_NOPE,))
    cosd, sda, sdb = _rope_tables(s, DIFF_ROPE // 2, ROPE_THETA, (0, DIFF_DK))
    return dict(cosm=cosm, sma=sma, smb=smb, cosd=cosd, sda=sda, sdb=sdb,
                bo128=_block_ones(LANES), bo64=_block_ones(DIFF_DK))


def _layer_params(l, p):
    f32 = jnp.float32
    cd = MXU_DTYPE
    w_in = p['w_in'][l]
    o_cq, o_ckv, o_kpe = 0, MLA_Q_LORA, MLA_Q_LORA + MLA_KV_LORA
    o_dq = o_kpe + MLA_ROPE
    o_dk = o_dq + DIFF_HEADS * 2 * DIFF_DK
    o_dv = o_dk + DIFF_HEADS * 2 * DIFF_DK
    kpe_pad = jnp.pad(w_in[:, o_kpe:o_dq], ((0, 0), (MLA_NOPE, LANES - MLA_QK)))
    w_in_main = jnp.concatenate([w_in[:, o_cq:o_kpe], kpe_pad, w_in[:, o_dq:o_dv]], axis=1)
    w_q = p['w_q_up'][l].reshape(MLA_Q_LORA, MLA_HEADS, MLA_QK)
    w_q = jnp.pad(w_q, ((0, 0), (0, 0), (0, LANES - MLA_QK))).reshape(MLA_Q_LORA, MLA_HEADS * LANES)
    w_kv = p['w_kv_up'][l].reshape(MLA_KV_LORA, MLA_HEADS, MLA_NOPE + MLA_V)
    w_k = jnp.pad(w_kv[:, :, :MLA_NOPE], ((0, 0), (0, 0), (0, LANES - MLA_NOPE)))
    w_k = w_k.reshape(MLA_KV_LORA, MLA_HEADS * LANES)
    w_vt = w_kv[:, :, MLA_NOPE:].reshape(MLA_KV_LORA, MLA_HEADS * MLA_V).T
    nchunk = D_FF // FFN_CHUNK
    conv = jnp.concatenate([p['conv_w'][l], p['conv_b'][l][None, :],
                            jnp.zeros((4, D_FF), f32)], axis=0)
    lam0 = 0.8 - 0.6 * math.exp(-0.3 * l)
    row = lambda v: v.astype(f32)[None, :]

    def score_bounded(gq, gk, width):
        bound = (width ** 0.5 * LOG2E * NORM_SLACK) * jnp.max(jnp.abs(gq)) * jnp.max(jnp.abs(gk))
        return (bound <= SCORE_BOUND_LOG2).astype(jnp.int32).reshape(1)

    w_out = p['w_out'][l]
    return dict(
        mla_bounded=score_bounded(p['mla_qn_g'][l], p['mla_kn_g'][l], MLA_QK),
        diff_bounded=score_bounded(p['diff_qn_g'][l], p['diff_kn_g'][l], DIFF_DK),
        ln1=row(p['ln1_g'][l]),
        w_in=w_in_main.astype(cd),
        w_dvt=w_in[:, o_dv:].T.astype(cd),
        g_cq=row(p['mla_q_norm_g'][l]),
        w_q=w_q.astype(cd),
        g_ckv=row(p['mla_kv_norm_g'][l]),
        w_k=w_k.astype(cd),
        w_vt=w_vt.astype(cd),
        g_qn=row(jnp.pad(p['mla_qn_g'][l], (0, LANES - MLA_QK))),
        g_kn=row(jnp.pad(p['mla_kn_g'][l], (0, LANES - MLA_QK))),
        g_dq=row(jnp.tile(p['diff_qn_g'][l], 2)),
        g_dk=row(jnp.tile(p['diff_kn_g'][l], 2)),
        lq1=row(p['lambda_q1'][l]), lk1=row(p['lambda_k1'][l]),
        lq2=row(p['lambda_q2'][l]), lk2=row(p['lambda_k2'][l]),
        lam0=jnp.full((1, 1), lam0, f32),
        g_sub=p['diff_subln_g'][l].astype(f32)[:, None],
        wo_m=w_out[:MLA_HEADS * MLA_V].astype(cd),
        wo_d=w_out[MLA_HEADS * MLA_V:].astype(cd),
        ln2=row(p['ln2_g'][l]),
        w_gate=p['w_gate'][l].reshape(D_MODEL, nchunk, FFN_CHUNK).transpose(1, 0, 2).astype(cd),
        w_up=p['w_up'][l].reshape(D_MODEL, nchunk, FFN_CHUNK).transpose(1, 0, 2).astype(cd),
        conv=conv.reshape(8, nchunk, FFN_CHUNK).transpose(1, 0, 2),
        w_down=p['w_down'][l].reshape(nchunk, FFN_CHUNK, D_MODEL).astype(cd),
    )


def _trunk(x, layers, tabs):
    b, s, _ = x.shape
    tq = min(ATT_TQ, s)
    tk = min(ATT_TK, s)
    tm_pre = min(PRE_TM, s)
    tm_ffn = min(FFN_TM, s)
    for lp in layers:
        qm, km, vmt, qd, kd, vdt = _pre_call(x, lp, tabs, tm_pre, min(VT_TILE, s))
        am = _mla_attn_call(lp['mla_bounded'], qm, km, vmt, tq, tk)
        ad = _diff_attn_call(lp['diff_bounded'], qd, kd, vdt, lp, tq, tk)
        x = _ffn_call(x, am, ad, lp, tm_ffn)
    return x


def kernel(x_prompt, x_sample, ln1_g, w_in, mla_q_norm_g, w_q_up, mla_kv_norm_g, w_kv_up,
           mla_qn_g, mla_kn_g, diff_qn_g, diff_kn_g, lambda_q1, lambda_k1, lambda_q2,
           lambda_k2, diff_subln_g, w_out, ln2_g, w_gate, conv_w, conv_b, w_up, w_down):
    params = dict(ln1_g=ln1_g, w_in=w_in, mla_q_norm_g=mla_q_norm_g, w_q_up=w_q_up,
                  mla_kv_norm_g=mla_kv_norm_g, w_kv_up=w_kv_up, mla_qn_g=mla_qn_g,
                  mla_kn_g=mla_kn_g, diff_qn_g=diff_qn_g, diff_kn_g=diff_kn_g,
                  lambda_q1=lambda_q1, lambda_k1=lambda_k1, lambda_q2=lambda_q2,
                  lambda_k2=lambda_k2, diff_subln_g=diff_subln_g, w_out=w_out, ln2_g=ln2_g,
                  w_gate=w_gate, conv_w=conv_w, conv_b=conv_b, w_up=w_up, w_down=w_down)
    layers = [_layer_params(l, params) for l in range(DEPTH)]
    tabs = {x.shape[1]: _tables(x.shape[1]) for x in (x_prompt, x_sample)}
    return tuple(_trunk(x, layers, tabs[x.shape[1]]) for x in (x_prompt, x_sample))
```

```python
import functools
import math

import jax
import jax.numpy as jnp
import numpy as np
from jax import lax
from jax.experimental import pallas as pl
from jax.experimental.pallas import tpu as pltpu

D_MODEL = 1024
DEPTH = 4
MLA_HEADS = 8
MLA_Q_LORA = 256
MLA_KV_LORA = 128
MLA_NOPE = 64
MLA_ROPE = 32
MLA_V = 64
MLA_QK = MLA_NOPE + MLA_ROPE
MLA_THETA = 10000.0
DIFF_HEADS = 4
DIFF_DK = 64
DIFF_DV = 2 * DIFF_DK
DIFF_ROPE = DIFF_DK // 4
ROPE_THETA = 500000.0
D_FF = 2816
EPS = 1e-6

LANES = 128
MXU_DTYPE = jnp.bfloat16
VMEM_LIMIT = 56 * 1024 * 1024

PRE_TM = 512
ATT_TQ = 2048
ATT_TQ_ONLINE = 1024
ATT_TK = 1024
VT_TILE = 512
ATT_SUB = 256
FFN_TM = 1024
FFN_HALO = 16
FFN_CHUNK = 256
NEG_BIG = -1e30
LOG2E = math.log2(math.e)
SCORE_BOUND_LOG2 = 60.0
NORM_SLACK = 1.05

COL_CKV = MLA_Q_LORA
COL_KPE = COL_CKV + MLA_KV_LORA
COL_DQ = COL_KPE + LANES
COL_DK = COL_DQ + DIFF_HEADS * 2 * DIFF_DK
COL_END = COL_DK + DIFF_HEADS * 2 * DIFF_DK
PAIR = 2 * LANES
MLA_W = MLA_HEADS * LANES
DIFF_QW = DIFF_HEADS * 2 * LANES
DIFF_KW = DIFF_HEADS * LANES
VT_ROWS = MLA_HEADS * MLA_V
assert VT_ROWS == DIFF_HEADS * DIFF_DV

_NT = (((1,), (1,)), ((), ()))


def _dot(a, b):
    return jnp.dot(a, b, preferred_element_type=jnp.float32)


def _dot_nt(a, b):
    return lax.dot_general(a, b, _NT, preferred_element_type=jnp.float32)


def _rms(x, g):
    ms = jnp.mean(x * x, axis=-1, keepdims=True)
    return x * lax.rsqrt(ms + EPS) * g


def _group_rsqrt(xc, block_ones, width):
    ssq = _dot((xc * xc).astype(MXU_DTYPE), block_ones)
    return lax.rsqrt(ssq * (1.0 / width) + EPS)


def _rope_group(xg, cos, sin_a, sin_b, shift):
    return (xg * cos + pltpu.roll(xg, LANES - shift, 1) * sin_a
            + pltpu.roll(xg, shift, 1) * sin_b)


def _pre_kernel(x_ref, ln1_ref, win_ref, wdvt_ref, gcq_ref, wq_ref, gckv_ref, wk_ref, wvt_ref,
                gqn_ref, gkn_ref, gdq_ref, gdk_ref,
                cosm_ref, sma_ref, smb_ref, cosd_ref, sda_ref, sdb_ref,
                bo128_ref, bo64_ref,
                qm_ref, km_ref, vmt_ref, qd_ref, kd_ref, vdt_ref):
    tm = x_ref.shape[1]
    tk = vmt_ref.shape[3]
    x = x_ref[0]
    hb = _rms(x, ln1_ref[...]).astype(MXU_DTYPE)
    proj = _dot(hb, win_ref[...])
    c_q = proj[:, 0:COL_CKV]
    c_kv = proj[:, COL_CKV:COL_KPE]
    kpe = proj[:, COL_KPE:COL_DQ]
    dq = proj[:, COL_DQ:COL_DK]
    dk = proj[:, COL_DK:COL_END]

    dvt = _dot_nt(wdvt_ref[...], hb).astype(MXU_DTYPE)
    cqn = _rms(c_q, gcq_ref[...]).astype(MXU_DTYPE)
    ckvn = _rms(c_kv, gckv_ref[...]).astype(MXU_DTYPE)
    q = _dot(cqn, wq_ref[...])
    kn = _dot(ckvn, wk_ref[...])
    vt = _dot_nt(wvt_ref[...], ckvn).astype(MXU_DTYPE)
    for t in range(tm // tk):
        vmt_ref[0, t] = vt[:, t * tk:(t + 1) * tk]
        vdt_ref[0, t] = dvt[:, t * tk:(t + 1) * tk]

    bo128 = bo128_ref[...]
    bo64 = bo64_ref[...]
    cosm, sma, smb = cosm_ref[...], sma_ref[...], smb_ref[...]
    cosd, sda, sdb = cosd_ref[...], sda_ref[...], sdb_ref[...]
    gqn, gkn, gdq, gdk = gqn_ref[...], gkn_ref[...], gdq_ref[...], gdk_ref[...]
    kpe2 = jnp.concatenate([kpe, kpe], axis=1)
    q_scale = MLA_QK ** -0.5 * LOG2E
    d_scale = DIFF_DK ** -0.5 * LOG2E
    half_m = MLA_ROPE // 2
    half_d = DIFF_ROPE // 2
    lane = lax.broadcasted_iota(jnp.int32, (tm, LANES), 1)

    for c in range(MLA_HEADS // 2):
        sl = slice(c * PAIR, (c + 1) * PAIR)
        qc = q[:, sl]
        qc = qc * _group_rsqrt(qc, bo128, MLA_QK)
        kc = kn[:, sl] + kpe2
        kc = kc * _group_rsqrt(kc, bo128, MLA_QK)
        for g in range(2):
            gs = slice(g * LANES, (g + 1) * LANES)
            out = slice((2 * c + g) * LANES, (2 * c + g + 1) * LANES)
            qg = _rope_group(qc[:, gs] * gqn, cosm, sma, smb, half_m)
            qm_ref[0, :, out] = (qg * q_scale).astype(MXU_DTYPE)
            kg = _rope_group(kc[:, gs] * gkn, cosm, sma, smb, half_m)
            km_ref[0, :, out] = kg.astype(MXU_DTYPE)

    for c in range(DIFF_HEADS // 2):
        sl = slice(c * PAIR, (c + 1) * PAIR)
        qc = dq[:, sl]
        qc = qc * _group_rsqrt(qc, bo64, DIFF_DK)
        kc = dk[:, sl]
        kc = kc * _group_rsqrt(kc, bo64, DIFF_DK)
        for g in range(2):
            h = 2 * c + g
            gs = slice(g * LANES, (g + 1) * LANES)
            qg = _rope_group(qc[:, gs] * gdq, cosd, sda, sdb, half_d) * d_scale
            zero = jnp.zeros_like(qg)
            qd_ref[0, :, (2 * h) * LANES:(2 * h + 1) * LANES] = (
                jnp.where(lane < DIFF_DK, qg, zero).astype(MXU_DTYPE))
            qd_ref[0, :, (2 * h + 1) * LANES:(2 * h + 2) * LANES] = (
                jnp.where(lane >= DIFF_DK, qg, zero).astype(MXU_DTYPE))
            kg = _rope_group(kc[:, gs] * gdk, cosd, sda, sdb, half_d)
            kd_ref[0, :, h * LANES:(h + 1) * LANES] = kg.astype(MXU_DTYPE)


def _pre_call(x, lp, tabs, tm, tk):
    b, s, d = x.shape
    nt = s // tm
    tpb = tm // tk

    def full(a):
        return pl.BlockSpec(a.shape, lambda bi, ti: (0,) * a.ndim)

    def tab_spec():
        return pl.BlockSpec((tm, LANES), lambda bi, ti: (ti, 0))

    weights = [lp['ln1'], lp['w_in'], lp['w_dvt'], lp['g_cq'], lp['w_q'], lp['g_ckv'], lp['w_k'],
               lp['w_vt'], lp['g_qn'], lp['g_kn'], lp['g_dq'], lp['g_dk']]
    consts = [tabs['bo128'], tabs['bo64']]
    rope = [tabs['cosm'], tabs['sma'], tabs['smb'], tabs['cosd'], tabs['sda'], tabs['sdb']]
    in_specs = ([pl.BlockSpec((1, tm, d), lambda bi, ti: (bi, ti, 0))]
                + [full(a) for a in weights] + [tab_spec() for _ in rope] + [full(a) for a in consts])
    tok = lambda w: pl.BlockSpec((1, tm, w), lambda bi, ti: (bi, ti, 0))
    vts = pl.BlockSpec((1, tpb, VT_ROWS, tk), lambda bi, ti: (bi, ti, 0, 0))
    out_shape = [
        jax.ShapeDtypeStruct((b, s, MLA_W), MXU_DTYPE),
        jax.ShapeDtypeStruct((b, s, MLA_W), MXU_DTYPE),
        jax.ShapeDtypeStruct((b, s // tk, VT_ROWS, tk), MXU_DTYPE),
        jax.ShapeDtypeStruct((b, s, DIFF_QW), MXU_DTYPE),
        jax.ShapeDtypeStruct((b, s, DIFF_KW), MXU_DTYPE),
        jax.ShapeDtypeStruct((b, s // tk, VT_ROWS, tk), MXU_DTYPE),
    ]
    out_specs = [tok(MLA_W), tok(MLA_W), vts, tok(DIFF_QW), tok(DIFF_KW), vts]
    return pl.pallas_call(
        _pre_kernel,
        grid=(b, nt),
        in_specs=in_specs,
        out_specs=out_specs,
        out_shape=out_shape,
        compiler_params=pltpu.CompilerParams(
            dimension_semantics=("arbitrary", "arbitrary"), vmem_limit_bytes=VMEM_LIMIT),
        name="proj",
    )(x, *weights, *rope, *consts)


def _flash_pipelined(nk, n_streams, qk_fn, pv_fn, s_scr, p_scr, tq, dvp):
    for st in range(n_streams):
        s_scr[st] = qk_fn(0, st)
        p_scr[st] = jnp.zeros(p_scr.shape[1:], p_scr.dtype)

    def body(j, carry):
        jn = jnp.minimum(j + 1, nk - 1)
        jp = jnp.maximum(j - 1, 0)
        out = []
        for st in range(n_streams):
            m, alpha, acc = carry[st]
            s_cur = s_scr[st]
            acc = acc * alpha + pv_fn(jp, st, p_scr[st])
            s_nxt = qk_fn(jn, st)
            m_new = jnp.maximum(m, jnp.max(s_cur, axis=0, keepdims=True))
            p_scr[st] = jnp.exp2(s_cur - m_new).astype(p_scr.dtype)
            s_scr[st] = s_nxt
            out.append((m_new, jnp.exp2(m - m_new), acc))
        return tuple(out)

    init = tuple((jnp.full((1, tq), NEG_BIG, jnp.float32), jnp.ones((1, tq), jnp.float32),
                  jnp.zeros((dvp, tq), jnp.float32)) for _ in range(n_streams))
    res = lax.fori_loop(0, nk, body, init)
    return [res[st][2] * res[st][1] + pv_fn(nk - 1, st, p_scr[st]) for st in range(n_streams)]


def _flash_bounded(nq, nk, n_streams, qk_fn, pv_fn, finish_fn, p_scr, acc_scr, qt_scr):
    total = nq * nk
    n_sub, tk = p_scr.shape[1:3]
    tq = n_sub * ATT_SUB
    dv = acc_scr.shape[2] - 16
    for st in range(n_streams):
        acc_scr[st] = jnp.zeros(acc_scr.shape[1:], acc_scr.dtype)

    def finish(qi):
        accs = [jnp.concatenate([acc_scr[st, i] for i in range(n_sub)], axis=1)
                for st in range(n_streams)]
        finish_fn(qi * tq, tq, [(a[:dv], jnp.sum(a[dv:dv + 8], axis=0, keepdims=True)) for a in accs])

    def pv_stage(j, i, st):
        acc_scr[st, i, :dv, :] += pv_fn(j, st, p_scr[st, i], False)

    def score_stage(t, i, st):
        if qt_scr is None:
            s = qk_fn((t // nk) * tq + i * ATT_SUB, ATT_SUB, t % nk, st)
        else:
            s = qk_fn(None, None, t % nk, st, qt_scr[st, i])
        p = jnp.exp2(s)
        p_scr[st, i] = p.astype(p_scr.dtype)
        return jnp.sum(p.reshape(tk // 8, 8, ATT_SUB), axis=0)

    def stage_queries(qi):
        for st in range(n_streams):
            for i in range(n_sub):
                q = qk_fn(qi * tq + i * ATT_SUB, ATT_SUB, None, st)
                qt_scr[st, i] = q.astype(jnp.float32).T.astype(qt_scr.dtype)

    def add_row_sums(psums):
        for st in range(n_streams):
            for i in range(n_sub):
                acc_scr[st, i, dv:dv + 8, :] += psums[st][i]

    if qt_scr is not None:
        stage_queries(0)
    add_row_sums([[score_stage(0, i, st) for i in range(n_sub)] for st in range(n_streams)])

    def body(t, c):
        jc = (t - 1) % nk
        psums = [[None] * n_sub for _ in range(n_streams)]
        for i in range(n_sub):
            for st in range(n_streams):
                pv_stage(jc, i, st)
                psums[st][i] = score_stage(t, i, st)

        @pl.when(jc == nk - 1)
        def _():
            finish((t - 1) // nk)
            for st in range(n_streams):
                acc_scr[st] = jnp.zeros(acc_scr.shape[1:], acc_scr.dtype)

        add_row_sums(psums)

        if qt_scr is not None:
            @pl.when(jnp.logical_and((t + 1) % nk == 0, t + 1 < total))
            def _():
                stage_queries((t + 1) // nk)

        return c

    lax.fori_loop(1, total, body, 0)
    for i in range(n_sub):
        for st in range(n_streams):
            pv_stage(nk - 1, i, st)
    finish(nq - 1)


def _rows(row0, width):
    return pl.ds(pl.multiple_of(row0, width), width)


def _vt_tile(vt_ref, j, tk, rows):
    tv = vt_ref.shape[3]
    n = tk // tv
    parts = [vt_ref[0, j * n + i, rows, :] for i in range(n)]
    return parts[0] if n == 1 else jnp.concatenate(parts, axis=1)


def _flash_scratch(tq, tk, dvp, stage_queries, n_streams=2):
    tq_on = min(tq, ATT_TQ_ONLINE)
    n_sub = tq // ATT_SUB
    scratch = [pltpu.VMEM((n_streams, tk, tq_on), jnp.float32),
               pltpu.VMEM((n_streams, tk, tq_on), MXU_DTYPE),
               pltpu.VMEM((n_streams, n_sub, tk, ATT_SUB), MXU_DTYPE),
               pltpu.VMEM((n_streams, n_sub, dvp, ATT_SUB), jnp.float32)]
    if stage_queries:
        scratch.append(pltpu.VMEM((n_streams, n_sub, LANES, ATT_SUB), MXU_DTYPE))
    return scratch


def _flash_either(bounded_ref, s_len, n_streams, qk_fn, pv_fn, finish_fn, scratch):
    s_scr, p_on_scr, p_scr, acc_scr = scratch[:4]
    qt_scr = scratch[4] if len(scratch) > 4 else None
    tk, tq_on = s_scr.shape[1:]
    tq = p_scr.shape[1] * ATT_SUB
    nk = s_len // tk
    dvp = acc_scr.shape[2]
    dv = dvp - 16

    @pl.when(bounded_ref[0] != 0)
    def _():
        _flash_bounded(s_len // tq, nk, n_streams, qk_fn, pv_fn, finish_fn, p_scr, acc_scr, qt_scr)

    @pl.when(bounded_ref[0] == 0)
    def _():
        def qbody(qi, c):
            accs = _flash_pipelined(nk, n_streams, lambda j, st: qk_fn(qi * tq_on, tq_on, j, st),
                                    lambda j, st, p: pv_fn(j, st, p, True),
                                    s_scr, p_on_scr, tq_on, dvp)
            finish_fn(qi * tq_on, tq_on, [(a[:dv], a[dv:dv + 1]) for a in accs])
            return c

        lax.fori_loop(0, s_len // tq_on, qbody, 0)


def _mla_attn_kernel(bounded_ref, q_ref, k_ref, vt_ref, o_ref, *scratch, tk):
    ones = jnp.ones((16, tk), MXU_DTYPE)
    dv = MLA_V

    def qk(row0, width, j, h, q_t=None):
        if j is None:
            return q_ref[0, _rows(row0, width), h * LANES:(h + 1) * LANES]
        k = k_ref[0, _rows(j * tk, tk), h * LANES:(h + 1) * LANES]
        if q_t is not None:
            return _dot(k, q_t)
        return _dot_nt(k, q_ref[0, _rows(row0, width), h * LANES:(h + 1) * LANES])

    def pv(j, h, p, with_ones):
        vt = _vt_tile(vt_ref, j, tk, slice(h * dv, (h + 1) * dv))
        return _dot(jnp.concatenate([vt, ones], axis=0) if with_ones else vt, p)

    def finish(row0, width, acc_l):
        o = jnp.concatenate([a / l for a, l in acc_l], axis=0)
        o_ref[0, _rows(row0, width), :] = o.T.astype(o_ref.dtype)

    _flash_either(bounded_ref, q_ref.shape[1], 2, qk, pv, finish, scratch)


def _mla_attn_call(bounded, qm, km, vmt, tq, tk):
    b, s, _ = qm.shape
    kern = functools.partial(_mla_attn_kernel, tk=tk)
    return pl.pallas_call(
        kern,
        grid=(b, MLA_HEADS // 2),
        in_specs=[
            pl.BlockSpec(memory_space=pltpu.SMEM),
            pl.BlockSpec((1, s, 2 * LANES), lambda bi, hp: (bi, 0, hp)),
            pl.BlockSpec((1, s, 2 * LANES), lambda bi, hp: (bi, 0, hp)),
            pl.BlockSpec((1, vmt.shape[1], 2 * MLA_V, vmt.shape[3]), lambda bi, hp: (bi, 0, hp, 0)),
        ],
        out_specs=pl.BlockSpec((1, s, 2 * MLA_V), lambda bi, hp: (bi, 0, hp)),
        out_shape=jax.ShapeDtypeStruct((b, s, MLA_HEADS * MLA_V), MXU_DTYPE),
        scratch_shapes=_flash_scratch(tq, tk, MLA_V + 16, stage_queries=True),
        compiler_params=pltpu.CompilerParams(
            dimension_semantics=("arbitrary", "arbitrary"), vmem_limit_bytes=VMEM_LIMIT),
        name="mla_attn",
    )(bounded, qm, km, vmt)


def _diff_attn_kernel(bounded_ref, q_ref, k_ref, vt_ref, lq1_ref, lk1_ref, lq2_ref, lk2_ref,
                      lam0_ref, g_ref, o_ref, *scratch, tk):
    ones = jnp.ones((16, tk), MXU_DTYPE)
    dv = DIFF_DV
    lam0 = lam0_ref[...]
    lam = (jnp.exp(jnp.sum(lq1_ref[...] * lk1_ref[...], axis=-1, keepdims=True))
           - jnp.exp(jnp.sum(lq2_ref[...] * lk2_ref[...], axis=-1, keepdims=True)) + lam0)
    gain = g_ref[...] * (1.0 - lam0)

    def qk(row0, width, j, mi):
        k = k_ref[0, _rows(j * tk, tk), :]
        return _dot_nt(k, q_ref[0, _rows(row0, width), mi * LANES:(mi + 1) * LANES])

    def pv(j, mi, p, with_ones):
        vt = _vt_tile(vt_ref, j, tk, slice(None))
        return _dot(jnp.concatenate([vt, ones], axis=0) if with_ones else vt, p)

    def finish(row0, width, acc_l):
        (a1, l1), (a2, l2) = acc_l
        o = a1 / l1 - lam * (a2 / l2)
        ms = jnp.mean(o * o, axis=0, keepdims=True)
        y = o * lax.rsqrt(ms + EPS) * gain
        o_ref[0, _rows(row0, width), :] = y.T.astype(o_ref.dtype)

    _flash_either(bounded_ref, q_ref.shape[1], 2, qk, pv, finish, scratch)


def _diff_attn_call(bounded, qd, kd, vdt, lp, tq, tk):
    b, s, _ = qd.shape
    kern = functools.partial(_diff_attn_kernel, tk=tk)
    small = [lp['lq1'], lp['lk1'], lp['lq2'], lp['lk2'], lp['lam0'], lp['g_sub']]
    return pl.pallas_call(
        kern,
        grid=(b, DIFF_HEADS),
        in_specs=[
            pl.BlockSpec(memory_space=pltpu.SMEM),
            pl.BlockSpec((1, s, 2 * LANES), lambda bi, h: (bi, 0, h)),
            pl.BlockSpec((1, s, LANES), lambda bi, h: (bi, 0, h)),
            pl.BlockSpec((1, vdt.shape[1], DIFF_DV, vdt.shape[3]), lambda bi, h: (bi, 0, h, 0)),
        ] + [pl.BlockSpec(a.shape, lambda bi, h: (0, 0)) for a in small],
        out_specs=pl.BlockSpec((1, s, DIFF_DV), lambda bi, h: (bi, 0, h)),
        out_shape=jax.ShapeDtypeStruct((b, s, DIFF_HEADS * DIFF_DV), MXU_DTYPE),
        scratch_shapes=_flash_scratch(tq, tk, DIFF_DV + 16, stage_queries=False),
        compiler_params=pltpu.CompilerParams(
            dimension_semantics=("arbitrary", "arbitrary"), vmem_limit_bytes=VMEM_LIMIT),
        name="diff_attn",
    )(bounded, qd, kd, vdt, *small)


def _ffn_kernel(x_ref, xp_ref, xn_ref, am_ref, amp_ref, amn_ref, ad_ref, adp_ref, adn_ref,
                wom_ref, wod_ref, ln2_ref, wg_ref, wu_ref, cw_ref, wd_ref,
                o_ref, h_scr):
    tm = x_ref.shape[1]
    halo = xp_ref.shape[1]
    ti = pl.program_id(1)
    nt = pl.num_programs(1)
    nchunk = wg_ref.shape[0]

    x_ext = jnp.concatenate([xp_ref[0], x_ref[0], xn_ref[0]], axis=0)
    am_ext = jnp.concatenate([amp_ref[0], am_ref[0], amn_ref[0]], axis=0)
    ad_ext = jnp.concatenate([adp_ref[0], ad_ref[0], adn_ref[0]], axis=0)
    xmid = x_ext + _dot(am_ext, wom_ref[...]) + _dot(ad_ext, wod_ref[...])
    h_scr[...] = _rms(xmid, ln2_ref[...]).astype(h_scr.dtype)
    o_ref[0] = xmid[halo:halo + tm]

    row = lax.broadcasted_iota(jnp.int32, (tm + 2 * halo, 1), 0)
    valid = jnp.logical_and(jnp.logical_or(row >= halo, ti > 0),
                            jnp.logical_or(row < halo + tm, ti < nt - 1))

    n_ext = tm + 2 * halo

    def chunk(c, carry):
        h_ext = h_scr[...]
        g = jnp.where(valid, _dot(h_ext, wg_ref[c]), 0.0)
        u = _dot(h_ext[halo:halo + tm], wu_ref[c])
        cw = cw_ref[c]
        g_prev = pltpu.roll(g, 1, 0)[halo:halo + tm]
        g_next = pltpu.roll(g, n_ext - 1, 0)[halo:halo + tm]
        conv = g_prev * cw[0:1] + g[halo:halo + tm] * cw[1:2] + g_next * cw[2:3] + cw[3:4]
        a = ((0.5 * conv) * (1.0 + jnp.tanh(0.5 * conv)) * u).astype(h_scr.dtype)
        o_ref[0] += _dot(a, wd_ref[c])
        return carry

    lax.fori_loop(0, nchunk, chunk, 0)


def _ffn_call(x, am, ad, lp, tm):
    b, s, d = x.shape
    nt = s // tm
    halo = FFN_HALO
    r = tm // halo
    nhb = s // halo

    def main(w):
        return pl.BlockSpec((1, tm, w), lambda bi, ti: (bi, ti, 0))

    def prev(w):
        return pl.BlockSpec((1, halo, w), lambda bi, ti: (bi, jnp.maximum(ti * r - 1, 0), 0))

    def nxt(w):
        return pl.BlockSpec((1, halo, w), lambda bi, ti: (bi, jnp.minimum((ti + 1) * r, nhb - 1), 0))

    def resident(a):
        return pl.BlockSpec(a.shape, lambda bi, ti: (0,) * a.ndim, pipeline_mode=pl.Buffered(1))

    weights = [lp['wo_m'], lp['wo_d'], lp['ln2'], lp['w_gate'], lp['w_up'], lp['conv'], lp['w_down']]
    wa = am.shape[-1]
    in_specs = ([main(d), prev(d), nxt(d), main(wa), prev(wa), nxt(wa), main(wa), prev(wa), nxt(wa)]
                + [resident(a) for a in weights])
    return pl.pallas_call(
        _ffn_kernel,
        grid=(b, nt),
        in_specs=in_specs,
        out_specs=main(d),
        out_shape=jax.ShapeDtypeStruct((b, s, d), jnp.float32),
        scratch_shapes=[pltpu.VMEM((tm + 2 * halo, d), MXU_DTYPE)],
        compiler_params=pltpu.CompilerParams(
            dimension_semantics=("arbitrary", "arbitrary"), vmem_limit_bytes=VMEM_LIMIT),
        name="ffn",
    )(x, x, x, am, am, am, ad, ad, ad, *weights)


def _rope_tables(s, half, theta, lane_starts):
    d = 2 * half
    freqs = theta ** (-jnp.arange(half, dtype=jnp.float32) * 2.0 / d)
    idx = np.zeros(LANES, np.int32)
    role = np.zeros(LANES, np.int32)
    for st in lane_starts:
        idx[st:st + d] = np.arange(d) % half
        role[st:st + half] = 1
        role[st + half:st + d] = 2
    ang = jnp.arange(s, dtype=jnp.float32)[:, None] * freqs[idx][None, :]
    cos, sin = jnp.cos(ang), jnp.sin(ang)
    role = jnp.asarray(role)[None, :]
    return (jnp.where(role > 0, cos, 1.0), jnp.where(role == 1, -sin, 0.0),
            jnp.where(role == 2, sin, 0.0))


def _block_ones(width):
    idx = np.arange(PAIR) // width
    return jnp.asarray(idx[:, None] == idx[None, :], MXU_DTYPE)


def _tables(s):
    cosm, sma, smb = _rope_tables(s, MLA_ROPE // 2, MLA_THETA, (MLA_NOPE,))
    cosd, sda, sdb = _rope_tables(s, DIFF_ROPE // 2, ROPE_THETA, (0, DIFF_DK))
    return dict(cosm=cosm, sma=sma, smb=smb, cosd=cosd, sda=sda, sdb=sdb,
                bo128=_block_ones(LANES), bo64=_block_ones(DIFF_DK))


def _layer_params(l, p):
    f32 = jnp.float32
    cd = MXU_DTYPE
    w_in = p['w_in'][l]
    o_cq, o_ckv, o_kpe = 0, MLA_Q_LORA, MLA_Q_LORA + MLA_KV_LORA
    o_dq = o_kpe + MLA_ROPE
    o_dk = o_dq + DIFF_HEADS * 2 * DIFF_DK
    o_dv = o_dk + DIFF_HEADS * 2 * DIFF_DK
    kpe_pad = jnp.pad(w_in[:, o_kpe:o_dq], ((0, 0), (MLA_NOPE, LANES - MLA_QK)))
    w_in_main = jnp.concatenate([w_in[:, o_cq:o_kpe], kpe_pad, w_in[:, o_dq:o_dv]], axis=1)
    w_q = p['w_q_up'][l].reshape(MLA_Q_LORA, MLA_HEADS, MLA_QK)
    w_q = jnp.pad(w_q, ((0, 0), (0, 0), (0, LANES - MLA_QK))).reshape(MLA_Q_LORA, MLA_HEADS * LANES)
    w_kv = p['w_kv_up'][l].reshape(MLA_KV_LORA, MLA_HEADS, MLA_NOPE + MLA_V)
    w_k = jnp.pad(w_kv[:, :, :MLA_NOPE], ((0, 0), (0, 0), (0, LANES - MLA_NOPE)))
    w_k = w_k.reshape(MLA_KV_LORA, MLA_HEADS * LANES)
    w_vt = w_kv[:, :, MLA_NOPE:].reshape(MLA_KV_LORA, MLA_HEADS * MLA_V).T
    nchunk = D_FF // FFN_CHUNK
    conv = jnp.concatenate([p['conv_w'][l], p['conv_b'][l][None, :],
                            jnp.zeros((4, D_FF), f32)], axis=0)
    lam0 = 0.8 - 0.6 * math.exp(-0.3 * l)
    row = lambda v: v.astype(f32)[None, :]

    def score_bounded(gq, gk, width):
        bound = (width ** 0.5 * LOG2E * NORM_SLACK) * jnp.max(jnp.abs(gq)) * jnp.max(jnp.abs(gk))
        return (bound <= SCORE_BOUND_LOG2).astype(jnp.int32).reshape(1)

    w_out = p['w_out'][l]
    return dict(
        mla_bounded=score_bounded(p['mla_qn_g'][l], p['mla_kn_g'][l], MLA_QK),
        diff_bounded=score_bounded(p['diff_qn_g'][l], p['diff_kn_g'][l], DIFF_DK),
        ln1=row(p['ln1_g'][l]),
        w_in=w_in_main.astype(cd),
        w_dvt=w_in[:, o_dv:].T.astype(cd),
        g_cq=row(p['mla_q_norm_g'][l]),
        w_q=w_q.astype(cd),
        g_ckv=row(p['mla_kv_norm_g'][l]),
        w_k=w_k.astype(cd),
        w_vt=w_vt.astype(cd),
        g_qn=row(jnp.pad(p['mla_qn_g'][l], (0, LANES - MLA_QK))),
        g_kn=row(jnp.pad(p['mla_kn_g'][l], (0, LANES - MLA_QK))),
        g_dq=row(jnp.tile(p['diff_qn_g'][l], 2)),
        g_dk=row(jnp.tile(p['diff_kn_g'][l], 2)),
        lq1=row(p['lambda_q1'][l]), lk1=row(p['lambda_k1'][l]),
        lq2=row(p['lambda_q2'][l]), lk2=row(p['lambda_k2'][l]),
        lam0=jnp.full((1, 1), lam0, f32),
        g_sub=p['diff_subln_g'][l].astype(f32)[:, None],
        wo_m=w_out[:MLA_HEADS * MLA_V].astype(cd),
        wo_d=w_out[MLA_HEADS * MLA_V:].astype(cd),
        ln2=row(p['ln2_g'][l]),
        w_gate=p['w_gate'][l].reshape(D_MODEL, nchunk, FFN_CHUNK).transpose(1, 0, 2).astype(cd),
        w_up=p['w_up'][l].reshape(D_MODEL, nchunk, FFN_CHUNK).transpose(1, 0, 2).astype(cd),
        conv=conv.reshape(8, nchunk, FFN_CHUNK).transpose(1, 0, 2),
        w_down=p['w_down'][l].reshape(nchunk, FFN_CHUNK, D_MODEL).astype(cd),
    )


def _trunk(x, layers, tabs):
    b, s, _ = x.shape
    tq = min(ATT_TQ, s)
    tk = min(ATT_TK, s)
    tm_pre = min(PRE_TM, s)
    tm_ffn = min(FFN_TM, s)
    for lp in layers:
        qm, km, vmt, qd, kd, vdt = _pre_call(x, lp, tabs, tm_pre, min(VT_TILE, s))
        am = _mla_attn_call(lp['mla_bounded'], qm, km, vmt, tq, tk)
        ad = _diff_attn_call(lp['diff_bounded'], qd, kd, vdt, lp, tq, tk)
        x = _ffn_call(x, am, ad, lp, tm_ffn)
    return x


def kernel(x_prompt, x_sample, ln1_g, w_in, mla_q_norm_g, w_q_up, mla_kv_norm_g, w_kv_up,
           mla_qn_g, mla_kn_g, diff_qn_g, diff_kn_g, lambda_q1, lambda_k1, lambda_q2,
           lambda_k2, diff_subln_g, w_out, ln2_g, w_gate, conv_w, conv_b, w_up, w_down):
    params = dict(ln1_g=ln1_g, w_in=w_in, mla_q_norm_g=mla_q_norm_g, w_q_up=w_q_up,
                  mla_kv_norm_g=mla_kv_norm_g, w_kv_up=w_kv_up, mla_qn_g=mla_qn_g,
                  mla_kn_g=mla_kn_g, diff_qn_g=diff_qn_g, diff_kn_g=diff_kn_g,
                  lambda_q1=lambda_q1, lambda_k1=lambda_k1, lambda_q2=lambda_q2,
                  lambda_k2=lambda_k2, diff_subln_g=diff_subln_g, w_out=w_out, ln2_g=ln2_g,
                  w_gate=w_gate, conv_w=conv_w, conv_b=conv_b, w_up=w_up, w_down=w_down)
    layers = [_layer_params(l, params) for l in range(DEPTH)]
    tabs = {x.shape[1]: _tables(x.shape[1]) for x in (x_prompt, x_sample)}
    return tuple(_trunk(x, layers, tabs[x.shape[1]]) for x in (x_prompt, x_sample))
```

```python
import functools
import math

import jax
import jax.numpy as jnp
import numpy as np
from jax import lax
from jax.experimental import pallas as pl
from jax.experimental.pallas import tpu as pltpu

D_MODEL = 1024
DEPTH = 4
MLA_HEADS = 8
MLA_Q_LORA = 256
MLA_KV_LORA = 128
MLA_NOPE = 64
MLA_ROPE = 32
MLA_V = 64
MLA_QK = MLA_NOPE + MLA_ROPE
MLA_THETA = 10000.0
DIFF_HEADS = 4
DIFF_DK = 64
DIFF_DV = 2 * DIFF_DK
DIFF_ROPE = DIFF_DK // 4
ROPE_THETA = 500000.0
D_FF = 2816
EPS = 1e-6

LANES = 128
MXU_DTYPE = jnp.bfloat16
VMEM_LIMIT = 56 * 1024 * 1024

PRE_TM = 512
ATT_TQ = 2048
ATT_TQ_ONLINE = 1024
ATT_TK = 1024
VT_TILE = 512
ATT_SUB = 256
FFN_TM = 1024
FFN_HALO = 16
FFN_CHUNK = 256
NEG_BIG = -1e30
LOG2E = math.log2(math.e)
SCORE_BOUND_LOG2 = 60.0
NORM_SLACK = 1.05

COL_CKV = MLA_Q_LORA
COL_KPE = COL_CKV + MLA_KV_LORA
COL_DQ = COL_KPE + LANES
COL_DK = COL_DQ + DIFF_HEADS * 2 * DIFF_DK
COL_END = COL_DK + DIFF_HEADS * 2 * DIFF_DK
PAIR = 2 * LANES
MLA_W = MLA_HEADS * LANES
DIFF_QW = DIFF_HEADS * 2 * LANES
DIFF_KW = DIFF_HEADS * LANES
VT_ROWS = MLA_HEADS * MLA_V
assert VT_ROWS == DIFF_HEADS * DIFF_DV

_NT = (((1,), (1,)), ((), ()))


def _dot(a, b):
    return jnp.dot(a, b, preferred_element_type=jnp.float32)


def _dot_nt(a, b):
    return lax.dot_general(a, b, _NT, preferred_element_type=jnp.float32)


def _rms(x, g):
    ms = jnp.mean(x * x, axis=-1, keepdims=True)
    return x * lax.rsqrt(ms + EPS) * g


def _group_rsqrt(xc, block_ones, width):
    ssq = _dot((xc * xc).astype(MXU_DTYPE), block_ones)
    return lax.rsqrt(ssq * (1.0 / width) + EPS)


def _rope_group(xg, cos, sin_a, sin_b, shift):
    return (xg * cos + pltpu.roll(xg, LANES - shift, 1) * sin_a
            + pltpu.roll(xg, shift, 1) * sin_b)


def _pre_kernel(x_ref, ln1_ref, win_ref, wdvt_ref, gcq_ref, wq_ref, gckv_ref, wk_ref, wvt_ref,
                gqn_ref, gkn_ref, gdq_ref, gdk_ref,
                cosm_ref, sma_ref, smb_ref, cosd_ref, sda_ref, sdb_ref,
                bo128_ref, bo64_ref,
                qm_ref, km_ref, vmt_ref, qd_ref, kd_ref, vdt_ref):
    tm = x_ref.shape[1]
    tk = vmt_ref.shape[3]
    x = x_ref[0]
    hb = _rms(x, ln1_ref[...]).astype(MXU_DTYPE)
    proj = _dot(hb, win_ref[...])
    c_q = proj[:, 0:COL_CKV]
    c_kv = proj[:, COL_CKV:COL_KPE]
    kpe = proj[:, COL_KPE:COL_DQ]
    dq = proj[:, COL_DQ:COL_DK]
    dk = proj[:, COL_DK:COL_END]

    dvt = _dot_nt(wdvt_ref[...], hb).astype(MXU_DTYPE)
    cqn = _rms(c_q, gcq_ref[...]).astype(MXU_DTYPE)
    ckvn = _rms(c_kv, gckv_ref[...]).astype(MXU_DTYPE)
    q = _dot(cqn, wq_ref[...])
    kn = _dot(ckvn, wk_ref[...])
    vt = _dot_nt(wvt_ref[...], ckvn).astype(MXU_DTYPE)
    for t in range(tm // tk):
        vmt_ref[0, t] = vt[:, t * tk:(t + 1) * tk]
        vdt_ref[0, t] = dvt[:, t * tk:(t + 1) * tk]

    bo128 = bo128_ref[...]
    bo64 = bo64_ref[...]
    cosm, sma, smb = cosm_ref[...], sma_ref[...], smb_ref[...]
    cosd, sda, sdb = cosd_ref[...], sda_ref[...], sdb_ref[...]
    gqn, gkn, gdq, gdk = gqn_ref[...], gkn_ref[...], gdq_ref[...], gdk_ref[...]
    kpe2 = jnp.concatenate([kpe, kpe], axis=1)
    q_scale = MLA_QK ** -0.5 * LOG2E
    d_scale = DIFF_DK ** -0.5 * LOG2E
    half_m = MLA_ROPE // 2
    half_d = DIFF_ROPE // 2
    lane = lax.broadcasted_iota(jnp.int32, (tm, LANES), 1)

    for c in range(MLA_HEADS // 2):
        sl = slice(c * PAIR, (c + 1) * PAIR)
        qc = q[:, sl]
        qc = qc * _group_rsqrt(qc, bo128, MLA_QK)
        kc = kn[:, sl] + kpe2
        kc = kc * _group_rsqrt(kc, bo128, MLA_QK)
        for g in range(2):
            gs = slice(g * LANES, (g + 1) * LANES)
            out = slice((2 * c + g) * LANES, (2 * c + g + 1) * LANES)
            qg = _rope_group(qc[:, gs] * gqn, cosm, sma, smb, half_m)
            qm_ref[0, :, out] = (qg * q_scale).astype(MXU_DTYPE)
            kg = _rope_group(kc[:, gs] * gkn, cosm, sma, smb, half_m)
            km_ref[0, :, out] = kg.astype(MXU_DTYPE)

    for c in range(DIFF_HEADS // 2):
        sl = slice(c * PAIR, (c + 1) * PAIR)
        qc = dq[:, sl]
        qc = qc * _group_rsqrt(qc, bo64, DIFF_DK)
        kc = dk[:, sl]
        kc = kc * _group_rsqrt(kc, bo64, DIFF_DK)
        for g in range(2):
            h = 2 * c + g
            gs = slice(g * LANES, (g + 1) * LANES)
            qg = _rope_group(qc[:, gs] * gdq, cosd, sda, sdb, half_d) * d_scale
            zero = jnp.zeros_like(qg)
            qd_ref[0, :, (2 * h) * LANES:(2 * h + 1) * LANES] = (
                jnp.where(lane < DIFF_DK, qg, zero).astype(MXU_DTYPE))
            qd_ref[0, :, (2 * h + 1) * LANES:(2 * h + 2) * LANES] = (
                jnp.where(lane >= DIFF_DK, qg, zero).astype(MXU_DTYPE))
            kg = _rope_group(kc[:, gs] * gdk, cosd, sda, sdb, half_d)
            kd_ref[0, :, h * LANES:(h + 1) * LANES] = kg.astype(MXU_DTYPE)


def _pre_call(x, lp, tabs, tm, tk):
    b, s, d = x.shape
    nt = s // tm
    tpb = tm // tk

    def full(a):
        return pl.BlockSpec(a.shape, lambda bi, ti: (0,) * a.ndim)

    def tab_spec():
        return pl.BlockSpec((tm, LANES), lambda bi, ti: (ti, 0))

    weights = [lp['ln1'], lp['w_in'], lp['w_dvt'], lp['g_cq'], lp['w_q'], lp['g_ckv'], lp['w_k'],
               lp['w_vt'], lp['g_qn'], lp['g_kn'], lp['g_dq'], lp['g_dk']]
    consts = [tabs['bo128'], tabs['bo64']]
    rope = [tabs['cosm'], tabs['sma'], tabs['smb'], tabs['cosd'], tabs['sda'], tabs['sdb']]
    in_specs = ([pl.BlockSpec((1, tm, d), lambda bi, ti: (bi, ti, 0))]
                + [full(a) for a in weights] + [tab_spec() for _ in rope] + [full(a) for a in consts])
    tok = lambda w: pl.BlockSpec((1, tm, w), lambda bi, ti: (bi, ti, 0))
    vts = pl.BlockSpec((1, tpb, VT_ROWS, tk), lambda bi, ti: (bi, ti, 0, 0))
    out_shape = [
        jax.ShapeDtypeStruct((b, s, MLA_W), MXU_DTYPE),
        jax.ShapeDtypeStruct((b, s, MLA_W), MXU_DTYPE),
        jax.ShapeDtypeStruct((b, s // tk, VT_ROWS, tk), MXU_DTYPE),
        jax.ShapeDtypeStruct((b, s, DIFF_QW), MXU_DTYPE),
        jax.ShapeDtypeStruct((b, s, DIFF_KW), MXU_DTYPE),
        jax.ShapeDtypeStruct((b, s // tk, VT_ROWS, tk), MXU_DTYPE),
    ]
    out_specs = [tok(MLA_W), tok(MLA_W), vts, tok(DIFF_QW), tok(DIFF_KW), vts]
    return pl.pallas_call(
        _pre_kernel,
        grid=(b, nt),
        in_specs=in_specs,
        out_specs=out_specs,
        out_shape=out_shape,
        compiler_params=pltpu.CompilerParams(
            dimension_semantics=("arbitrary", "arbitrary"), vmem_limit_bytes=VMEM_LIMIT),
        name="proj",
    )(x, *weights, *rope, *consts)


def _flash_pipelined(nk, n_streams, qk_fn, pv_fn, s_scr, p_scr, tq, dvp):
    for st in range(n_streams):
        s_scr[st] = qk_fn(0, st)
        p_scr[st] = jnp.zeros(p_scr.shape[1:], p_scr.dtype)

    def body(j, carry):
        jn = jnp.minimum(j + 1, nk - 1)
        jp = jnp.maximum(j - 1, 0)
        out = []
        for st in range(n_streams):
            m, alpha, acc = carry[st]
            s_cur = s_scr[st]
            acc = acc * alpha + pv_fn(jp, st, p_scr[st])
            s_nxt = qk_fn(jn, st)
            m_new = jnp.maximum(m, jnp.max(s_cur, axis=0, keepdims=True))
            p_scr[st] = jnp.exp2(s_cur - m_new).astype(p_scr.dtype)
            s_scr[st] = s_nxt
            out.append((m_new, jnp.exp2(m - m_new), acc))
        return tuple(out)

    init = tuple((jnp.full((1, tq), NEG_BIG, jnp.float32), jnp.ones((1, tq), jnp.float32),
                  jnp.zeros((dvp, tq), jnp.float32)) for _ in range(n_streams))
    res = lax.fori_loop(0, nk, body, init)
    return [res[st][2] * res[st][1] + pv_fn(nk - 1, st, p_scr[st]) for st in range(n_streams)]


def _flash_bounded(nq, nk, n_streams, qk_fn, pv_fn, finish_fn, p_scr, acc_scr, qt_scr):
    total = nq * nk
    n_sub, tk = p_scr.shape[1:3]
    tq = n_sub * ATT_SUB
    dv = acc_scr.shape[2] - 16
    for st in range(n_streams):
        acc_scr[st] = jnp.zeros(acc_scr.shape[1:], acc_scr.dtype)

    def finish(qi):
        accs = [jnp.concatenate([acc_scr[st, i] for i in range(n_sub)], axis=1)
                for st in range(n_streams)]
        finish_fn(qi * tq, tq, [(a[:dv], jnp.sum(a[dv:dv + 8], axis=0, keepdims=True)) for a in accs])

    def pv_stage(j, i, st):
        acc_scr[st, i, :dv, :] += pv_fn(j, st, p_scr[st, i], False)

    def score_stage(t, i, st):
        if qt_scr is None:
            s = qk_fn((t // nk) * tq + i * ATT_SUB, ATT_SUB, t % nk, st)
        else:
            s = qk_fn(None, None, t % nk, st, qt_scr[st, i])
        p = jnp.exp2(s)
        p_scr[st, i] = p.astype(p_scr.dtype)
        return jnp.sum(p.reshape(tk // 8, 8, ATT_SUB), axis=0)

    def stage_queries(qi):
        for st in range(n_streams):
            for i in range(n_sub):
                q = qk_fn(qi * tq + i * ATT_SUB, ATT_SUB, None, st)
                qt_scr[st, i] = q.astype(jnp.float32).T.astype(qt_scr.dtype)

    def add_row_sums(psums):
        for st in range(n_streams):
            for i in range(n_sub):
                acc_scr[st, i, dv:dv + 8, :] += psums[st][i]

    if qt_scr is not None:
        stage_queries(0)
    add_row_sums([[score_stage(0, i, st) for i in range(n_sub)] for st in range(n_streams)])

    def body(t, c):
        jc = (t - 1) % nk
        psums = [[None] * n_sub for _ in range(n_streams)]
        for i in range(n_sub):
            for st in range(n_streams):
                pv_stage(jc, i, st)
                psums[st][i] = score_stage(t, i, st)

        @pl.when(jc == nk - 1)
        def _():
            finish((t - 1) // nk)
            for st in range(n_streams):
                acc_scr[st] = jnp.zeros(acc_scr.shape[1:], acc_scr.dtype)

        add_row_sums(psums)

        if qt_scr is not None:
            @pl.when(jnp.logical_and((t + 1) % nk == 0, t + 1 < total))
            def _():
                stage_queries((t + 1) // nk)

        return c

    lax.fori_loop(1, total, body, 0)
    for i in range(n_sub):
        for st in range(n_streams):
            pv_stage(nk - 1, i, st)
    finish(nq - 1)


def _rows(row0, width):
    return pl.ds(pl.multiple_of(row0, width), width)


def _vt_tile(vt_ref, j, tk, rows):
    tv = vt_ref.shape[3]
    n = tk // tv
    parts = [vt_ref[0, j * n + i, rows, :] for i in range(n)]
    return parts[0] if n == 1 else jnp.concatenate(parts, axis=1)


def _flash_scratch(tq, tk, dvp, stage_queries, n_streams=2):
    tq_on = min(tq, ATT_TQ_ONLINE)
    n_sub = tq // ATT_SUB
    scratch = [pltpu.VMEM((n_streams, tk, tq_on), jnp.float32),
               pltpu.VMEM((n_streams, tk, tq_on), MXU_DTYPE),
               pltpu.VMEM((n_streams, n_sub, tk, ATT_SUB), MXU_DTYPE),
               pltpu.VMEM((n_streams, n_sub, dvp, ATT_SUB), jnp.float32)]
    if stage_queries:
        scratch.append(pltpu.VMEM((n_streams, n_sub, LANES, ATT_SUB), MXU_DTYPE))
    return scratch


def _flash_either(bounded_ref, s_len, n_streams, qk_fn, pv_fn, finish_fn, scratch):
    s_scr, p_on_scr, p_scr, acc_scr = scratch[:4]
    qt_scr = scratch[4] if len(scratch) > 4 else None
    tk, tq_on = s_scr.shape[1:]
    tq = p_scr.shape[1] * ATT_SUB
    nk = s_len // tk
    dvp = acc_scr.shape[2]
    dv = dvp - 16

    @pl.when(bounded_ref[0] != 0)
    def _():
        _flash_bounded(s_len // tq, nk, n_streams, qk_fn, pv_fn, finish_fn, p_scr, acc_scr, qt_scr)

    @pl.when(bounded_ref[0] == 0)
    def _():
        def qbody(qi, c):
            accs = _flash_pipelined(nk, n_streams, lambda j, st: qk_fn(qi * tq_on, tq_on, j, st),
                                    lambda j, st, p: pv_fn(j, st, p, True),
                                    s_scr, p_on_scr, tq_on, dvp)
            finish_fn(qi * tq_on, tq_on, [(a[:dv], a[dv:dv + 1]) for a in accs])
            return c

        lax.fori_loop(0, s_len // tq_on, qbody, 0)


def _mla_attn_kernel(bounded_ref, q_ref, k_ref, vt_ref, o_ref, *scratch, tk):
    ones = jnp.ones((16, tk), MXU_DTYPE)
    dv = MLA_V

    def qk(row0, width, j, h, q_t=None):
        if j is None:
            return q_ref[0, _rows(row0, width), h * LANES:(h + 1) * LANES]
        k = k_ref[0, _rows(j * tk, tk), h * LANES:(h + 1) * LANES]
        if q_t is not None:
            return _dot(k, q_t)
        return _dot_nt(k, q_ref[0, _rows(row0, width), h * LANES:(h + 1) * LANES])

    def pv(j, h, p, with_ones):
        vt = _vt_tile(vt_ref, j, tk, slice(h * dv, (h + 1) * dv))
        return _dot(jnp.concatenate([vt, ones], axis=0) if with_ones else vt, p)

    def finish(row0, width, acc_l):
        o = jnp.concatenate([a / l for a, l in acc_l], axis=0)
        o_ref[0, _rows(row0, width), :] = o.T.astype(o_ref.dtype)

    _flash_either(bounded_ref, q_ref.shape[1], 2, qk, pv, finish, scratch)


def _mla_attn_call(bounded, qm, km, vmt, tq, tk):
    b, s, _ = qm.shape
    kern = functools.partial(_mla_attn_kernel, tk=tk)
    return pl.pallas_call(
        kern,
        grid=(b, MLA_HEADS // 2),
        in_specs=[
            pl.BlockSpec(memory_space=pltpu.SMEM),
            pl.BlockSpec((1, s, 2 * LANES), lambda bi, hp: (bi, 0, hp)),
            pl.BlockSpec((1, s, 2 * LANES), lambda bi, hp: (bi, 0, hp)),
            pl.BlockSpec((1, vmt.shape[1], 2 * MLA_V, vmt.shape[3]), lambda bi, hp: (bi, 0, hp, 0)),
        ],
        out_specs=pl.BlockSpec((1, s, 2 * MLA_V), lambda bi, hp: (bi, 0, hp)),
        out_shape=jax.ShapeDtypeStruct((b, s, MLA_HEADS * MLA_V), MXU_DTYPE),
        scratch_shapes=_flash_scratch(tq, tk, MLA_V + 16, stage_queries=True),
        compiler_params=pltpu.CompilerParams(
            dimension_semantics=("arbitrary", "arbitrary"), vmem_limit_bytes=VMEM_LIMIT),
        name="mla_attn",
    )(bounded, qm, km, vmt)


def _diff_attn_kernel(bounded_ref, q_ref, k_ref, vt_ref, lq1_ref, lk1_ref, lq2_ref, lk2_ref,
                      lam0_ref, g_ref, o_ref, *scratch, tk):
    ones = jnp.ones((16, tk), MXU_DTYPE)
    dv = DIFF_DV
    lam0 = lam0_ref[...]
    lam = (jnp.exp(jnp.sum(lq1_ref[...] * lk1_ref[...], axis=-1, keepdims=True))
           - jnp.exp(jnp.sum(lq2_ref[...] * lk2_ref[...], axis=-1, keepdims=True)) + lam0)
    gain = g_ref[...] * (1.0 - lam0)

    def qk(row0, width, j, mi):
        k = k_ref[0, _rows(j * tk, tk), :]
        return _dot_nt(k, q_ref[0, _rows(row0, width), mi * LANES:(mi + 1) * LANES])

    def pv(j, mi, p, with_ones):
        vt = _vt_tile(vt_ref, j, tk, slice(None))
        return _dot(jnp.concatenate([vt, ones], axis=0) if with_ones else vt, p)

    def finish(row0, width, acc_l):
        (a1, l1), (a2, l2) = acc_l
        o = a1 / l1 - lam * (a2 / l2)
        ms = jnp.mean(o * o, axis=0, keepdims=True)
        y = o * lax.rsqrt(ms + EPS) * gain
        o_ref[0, _rows(row0, width), :] = y.T.astype(o_ref.dtype)

    _flash_either(bounded_ref, q_ref.shape[1], 2, qk, pv, finish, scratch)


def _diff_attn_call(bounded, qd, kd, vdt, lp, tq, tk):
    b, s, _ = qd.shape
    kern = functools.partial(_diff_attn_kernel, tk=tk)
    small = [lp['lq1'], lp['lk1'], lp['lq2'], lp['lk2'], lp['lam0'], lp['g_sub']]
    return pl.pallas_call(
        kern,
        grid=(b, DIFF_HEADS),
        in_specs=[
            pl.BlockSpec(memory_space=pltpu.SMEM),
            pl.BlockSpec((1, s, 2 * LANES), lambda bi, h: (bi, 0, h)),
            pl.BlockSpec((1, s, LANES), lambda bi, h: (bi, 0, h)),
            pl.BlockSpec((1, vdt.shape[1], DIFF_DV, vdt.shape[3]), lambda bi, h: (bi, 0, h, 0)),
        ] + [pl.BlockSpec(a.shape, lambda bi, h: (0, 0)) for a in small],
        out_specs=pl.BlockSpec((1, s, DIFF_DV), lambda bi, h: (bi, 0, h)),
        out_shape=jax.ShapeDtypeStruct((b, s, DIFF_HEADS * DIFF_DV), MXU_DTYPE),
        scratch_shapes=_flash_scratch(tq, tk, DIFF_DV + 16, stage_queries=False),
        compiler_params=pltpu.CompilerParams(
            dimension_semantics=("arbitrary", "arbitrary"), vmem_limit_bytes=VMEM_LIMIT),
        name="diff_attn",
    )(bounded, qd, kd, vdt, *small)


def _ffn_kernel(x_ref, xp_ref, xn_ref, am_ref, amp_ref, amn_ref, ad_ref, adp_ref, adn_ref,
                wom_ref, wod_ref, ln2_ref, wgu_ref, cw_ref, wd_ref,
                o_ref, h_scr):
    tm = x_ref.shape[1]
    halo = xp_ref.shape[1]
    ti = pl.program_id(1)
    nt = pl.num_programs(1)
    nchunk = wgu_ref.shape[0]

    x_ext = jnp.concatenate([xp_ref[0], x_ref[0], xn_ref[0]], axis=0)
    am_ext = jnp.concatenate([amp_ref[0], am_ref[0], amn_ref[0]], axis=0)
    ad_ext = jnp.concatenate([adp_ref[0], ad_ref[0], adn_ref[0]], axis=0)
    xmid = x_ext + _dot(am_ext, wom_ref[...]) + _dot(ad_ext, wod_ref[...])
    h_scr[...] = _rms(xmid, ln2_ref[...]).astype(h_scr.dtype)
    o_ref[0] = xmid[halo:halo + tm]

    row = lax.broadcasted_iota(jnp.int32, (tm + 2 * halo, 1), 0)
    valid = jnp.logical_and(jnp.logical_or(row >= halo, ti > 0),
                            jnp.logical_or(row < halo + tm, ti < nt - 1))

    n_ext = tm + 2 * halo

    def chunk(c, carry):
        h_ext = h_scr[...]
        gu = _dot(h_ext, wgu_ref[c])
        g = jnp.where(valid, gu[:, :FFN_CHUNK], 0.0)
        u = gu[halo:halo + tm, FFN_CHUNK:]
        cw = cw_ref[c]
        g_prev = pltpu.roll(g, 1, 0)[halo:halo + tm]
        g_next = pltpu.roll(g, n_ext - 1, 0)[halo:halo + tm]
        conv = g_prev * cw[0:1] + g[halo:halo + tm] * cw[1:2] + g_next * cw[2:3] + cw[3:4]
        a = ((0.5 * conv) * (1.0 + jnp.tanh(0.5 * conv)) * u).astype(h_scr.dtype)
        o_ref[0] += _dot(a, wd_ref[c])
        return carry

    lax.fori_loop(0, nchunk, chunk, 0)


def _ffn_call(x, am, ad, lp, tm):
    b, s, d = x.shape
    nt = s // tm
    halo = FFN_HALO
    r = tm // halo
    nhb = s // halo

    def main(w):
        return pl.BlockSpec((1, tm, w), lambda bi, ti: (bi, ti, 0))

    def prev(w):
        return pl.BlockSpec((1, halo, w), lambda bi, ti: (bi, jnp.maximum(ti * r - 1, 0), 0))

    def nxt(w):
        return pl.BlockSpec((1, halo, w), lambda bi, ti: (bi, jnp.minimum((ti + 1) * r, nhb - 1), 0))

    def resident(a):
        return pl.BlockSpec(a.shape, lambda bi, ti: (0,) * a.ndim, pipeline_mode=pl.Buffered(1))

    weights = [lp['wo_m'], lp['wo_d'], lp['ln2'], lp['w_gate_up'], lp['conv'], lp['w_down']]
    wa = am.shape[-1]
    in_specs = ([main(d), prev(d), nxt(d), main(wa), prev(wa), nxt(wa), main(wa), prev(wa), nxt(wa)]
                + [resident(a) for a in weights])
    return pl.pallas_call(
        _ffn_kernel,
        grid=(b, nt),
        in_specs=in_specs,
        out_specs=main(d),
        out_shape=jax.ShapeDtypeStruct((b, s, d), jnp.float32),
        scratch_shapes=[pltpu.VMEM((tm + 2 * halo, d), MXU_DTYPE)],
        compiler_params=pltpu.CompilerParams(
            dimension_semantics=("arbitrary", "arbitrary"), vmem_limit_bytes=VMEM_LIMIT),
        name="ffn",
    )(x, x, x, am, am, am, ad, ad, ad, *weights)


def _rope_tables(s, half, theta, lane_starts):
    d = 2 * half
    freqs = theta ** (-jnp.arange(half, dtype=jnp.float32) * 2.0 / d)
    idx = np.zeros(LANES, np.int32)
    role = np.zeros(LANES, np.int32)
    for st in lane_starts:
        idx[st:st + d] = np.arange(d) % half
        role[st:st + half] = 1
        role[st + half:st + d] = 2
    ang = jnp.arange(s, dtype=jnp.float32)[:, None] * freqs[idx][None, :]
    cos, sin = jnp.cos(ang), jnp.sin(ang)
    role = jnp.asarray(role)[None, :]
    return (jnp.where(role > 0, cos, 1.0), jnp.where(role == 1, -sin, 0.0),
            jnp.where(role == 2, sin, 0.0))


def _block_ones(width):
    idx = np.arange(PAIR) // width
    return jnp.asarray(idx[:, None] == idx[None, :], MXU_DTYPE)


def _tables(s):
    cosm, sma, smb = _rope_tables(s, MLA_ROPE // 2, MLA_THETA, (MLA_NOPE,))
    cosd, sda, sdb = _rope_tables(s, DIFF_ROPE // 2, ROPE_THETA, (0, DIFF_DK))
    return dict(cosm=cosm, sma=sma, smb=smb, cosd=cosd, sda=sda, sdb=sdb,
                bo128=_block_ones(LANES), bo64=_block_ones(DIFF_DK))


def _layer_params(l, p):
    f32 = jnp.float32
    cd = MXU_DTYPE
    w_in = p['w_in'][l]
    o_cq, o_ckv, o_kpe = 0, MLA_Q_LORA, MLA_Q_LORA + MLA_KV_LORA
    o_dq = o_kpe + MLA_ROPE
    o_dk = o_dq + DIFF_HEADS * 2 * DIFF_DK
    o_dv = o_dk + DIFF_HEADS * 2 * DIFF_DK
    kpe_pad = jnp.pad(w_in[:, o_kpe:o_dq], ((0, 0), (MLA_NOPE, LANES - MLA_QK)))
    w_in_main = jnp.concatenate([w_in[:, o_cq:o_kpe], kpe_pad, w_in[:, o_dq:o_dv]], axis=1)
    w_q = p['w_q_up'][l].reshape(MLA_Q_LORA, MLA_HEADS, MLA_QK)
    w_q = jnp.pad(w_q, ((0, 0), (0, 0), (0, LANES - MLA_QK))).reshape(MLA_Q_LORA, MLA_HEADS * LANES)
    w_kv = p['w_kv_up'][l].reshape(MLA_KV_LORA, MLA_HEADS, MLA_NOPE + MLA_V)
    w_k = jnp.pad(w_kv[:, :, :MLA_NOPE], ((0, 0), (0, 0), (0, LANES - MLA_NOPE)))
    w_k = w_k.reshape(MLA_KV_LORA, MLA_HEADS * LANES)
    w_vt = w_kv[:, :, MLA_NOPE:].reshape(MLA_KV_LORA, MLA_HEADS * MLA_V).T
    nchunk = D_FF // FFN_CHUNK
    conv = jnp.concatenate([p['conv_w'][l], p['conv_b'][l][None, :],
                            jnp.zeros((4, D_FF), f32)], axis=0)
    lam0 = 0.8 - 0.6 * math.exp(-0.3 * l)
    row = lambda v: v.astype(f32)[None, :]

    def score_bounded(gq, gk, width):
        bound = (width ** 0.5 * LOG2E * NORM_SLACK) * jnp.max(jnp.abs(gq)) * jnp.max(jnp.abs(gk))
        return (bound <= SCORE_BOUND_LOG2).astype(jnp.int32).reshape(1)

    w_out = p['w_out'][l]
    return dict(
        mla_bounded=score_bounded(p['mla_qn_g'][l], p['mla_kn_g'][l], MLA_QK),
        diff_bounded=score_bounded(p['diff_qn_g'][l], p['diff_kn_g'][l], DIFF_DK),
        ln1=row(p['ln1_g'][l]),
        w_in=w_in_main.astype(cd),
        w_dvt=w_in[:, o_dv:].T.astype(cd),
        g_cq=row(p['mla_q_norm_g'][l]),
        w_q=w_q.astype(cd),
        g_ckv=row(p['mla_kv_norm_g'][l]),
        w_k=w_k.astype(cd),
        w_vt=w_vt.astype(cd),
        g_qn=row(jnp.pad(p['mla_qn_g'][l], (0, LANES - MLA_QK))),
        g_kn=row(jnp.pad(p['mla_kn_g'][l], (0, LANES - MLA_QK))),
        g_dq=row(jnp.tile(p['diff_qn_g'][l], 2)),
        g_dk=row(jnp.tile(p['diff_kn_g'][l], 2)),
        lq1=row(p['lambda_q1'][l]), lk1=row(p['lambda_k1'][l]),
        lq2=row(p['lambda_q2'][l]), lk2=row(p['lambda_k2'][l]),
        lam0=jnp.full((1, 1), lam0, f32),
        g_sub=p['diff_subln_g'][l].astype(f32)[:, None],
        wo_m=w_out[:MLA_HEADS * MLA_V].astype(cd),
        wo_d=w_out[MLA_HEADS * MLA_V:].astype(cd),
        ln2=row(p['ln2_g'][l]),
        w_gate_up=jnp.concatenate(
            [p['w_gate'][l].reshape(D_MODEL, nchunk, FFN_CHUNK),
             p['w_up'][l].reshape(D_MODEL, nchunk, FFN_CHUNK)], axis=2).transpose(1, 0, 2).astype(cd),
        conv=conv.reshape(8, nchunk, FFN_CHUNK).transpose(1, 0, 2),
        w_down=p['w_down'][l].reshape(nchunk, FFN_CHUNK, D_MODEL).astype(cd),
    )


def _trunk(x, layers, tabs):
    b, s, _ = x.shape
    tq = min(ATT_TQ, s)
    tk = min(ATT_TK, s)
    tm_pre = min(PRE_TM, s)
    tm_ffn = min(FFN_TM, s)
    for lp in layers:
        qm, km, vmt, qd, kd, vdt = _pre_call(x, lp, tabs, tm_pre, min(VT_TILE, s))
        am = _mla_attn_call(lp['mla_bounded'], qm, km, vmt, tq, tk)
        ad = _diff_attn_call(lp['diff_bounded'], qd, kd, vdt, lp, tq, tk)
        x = _ffn_call(x, am, ad, lp, tm_ffn)
    return x


def kernel(x_prompt, x_sample, ln1_g, w_in, mla_q_norm_g, w_q_up, mla_kv_norm_g, w_kv_up,
           mla_qn_g, mla_kn_g, diff_qn_g, diff_kn_g, lambda_q1, lambda_k1, lambda_q2,
           lambda_k2, diff_subln_g, w_out, ln2_g, w_gate, conv_w, conv_b, w_up, w_down):
    params = dict(ln1_g=ln1_g, w_in=w_in, mla_q_norm_g=mla_q_norm_g, w_q_up=w_q_up,
                  mla_kv_norm_g=mla_kv_norm_g, w_kv_up=w_kv_up, mla_qn_g=mla_qn_g,
                  mla_kn_g=mla_kn_g, diff_qn_g=diff_qn_g, diff_kn_g=diff_kn_g,
                  lambda_q1=lambda_q1, lambda_k1=lambda_k1, lambda_q2=lambda_q2,
                  lambda_k2=lambda_k2, diff_subln_g=diff_subln_g, w_out=w_out, ln2_g=ln2_g,
                  w_gate=w_gate, conv_w=conv_w, conv_b=conv_b, w_up=w_up, w_down=w_down)
    layers = [_layer_params(l, params) for l in range(DEPTH)]
    tabs = {x.shape[1]: _tables(x.shape[1]) for x in (x_prompt, x_sample)}
    return tuple(_trunk(x, layers, tabs[x.shape[1]]) for x in (x_prompt, x_sample))
```
